```python
import jax, jax.numpy as jnp
from jax import lax
import numpy as np

D_MODEL = 1024
BATCH = 4
SEQ = 4096
DEPTH = 2
DEC_BATCH = 128
DEC_SEQ = 8
PAST_LEN = 2048
PAGE_SIZE = 128

POOL_WIDTH = D_MODEL // 4
POOL_WINDOWS = (2, 4, 8, 16)
POOL_GROUP_W = POOL_WIDTH // len(POOL_WINDOWS)
POOL_KEEP = max(POOL_WINDOWS) - 1
GMLP_WIDTH = D_MODEL // 4
GMLP_GROUPS = 4
GMLP_GROUP_W = GMLP_WIDTH // GMLP_GROUPS
CHUNK = 128
N_HEADS = 8
HEAD_DIM = 64
N_KV_HEADS = 2
GQA = N_HEADS // N_KV_HEADS
NSA_WIDTH = N_HEADS * HEAD_DIM
ROT_DIM = HEAD_DIM // 4
ROPE_THETA = 500000.0
CMP_LEN = 32
CMP_STRIDE = 16
CMP_HIDDEN = 2 * HEAD_DIM
SEL_LEN = 64
SEL_TOPN = 16
WINDOW = 512
Q_BLOCK = 128
N_EXPERTS = 16
N_EXPERT_GROUPS = 4
EXPERTS_PER_GROUP = N_EXPERTS // N_EXPERT_GROUPS
TOP_K = 2
D_EXPERT = D_MODEL // 4
DN_ALPHA = (2 * DEPTH) ** 0.25
DN_BETA = (8 * DEPTH) ** -0.25
LN_EPS = 1e-5
NEG_INF = -1e30
FORCE = 1e9
IN_SPLITS = (POOL_WIDTH, GMLP_WIDTH, GMLP_WIDTH, NSA_WIDTH, 6 * N_KV_HEADS * HEAD_DIM, 3 * N_HEADS, 3 * D_MODEL)
IN_WIDTH = sum(IN_SPLITS)

kernel_name = "hybrid_pool_gmlp_nsa_moe_step"


def split_points():
    pts, acc = [], 0
    for w in IN_SPLITS[:-1]:
        acc += w
        pts.append(acc)
    return pts


def layer_norm(x, g, b):
    xf = x.astype(jnp.float32)
    mu = jnp.mean(xf, -1, keepdims=True)
    var = jnp.mean(jnp.square(xf - mu), -1, keepdims=True)
    y = (xf - mu) * lax.rsqrt(var + LN_EPS) * g.astype(jnp.float32) + b.astype(jnp.float32)
    return y.astype(x.dtype)


def masked_softmax(s, mask):
    s = jnp.where(mask, s.astype(jnp.float32), NEG_INF)
    p = jnp.where(mask, jnp.exp(s - jnp.max(s, -1, keepdims=True)), 0.0)
    d = jnp.sum(p, -1, keepdims=True)
    return p / jnp.where(d > 0, d, 1.0)


def partial_rope(x, pos):
    half = ROT_DIM // 2
    inv = 1.0 / (ROPE_THETA ** (jnp.arange(half, dtype=jnp.float32) / half))
    ang = pos.astype(jnp.float32)[:, None] * inv[None, :]
    cos = jnp.cos(ang)[None, :, None, :]
    sin = jnp.sin(ang)[None, :, None, :]
    xr = x[..., :ROT_DIM].astype(jnp.float32)
    x1, x2 = xr[..., :half], xr[..., half:]
    rot = jnp.concatenate([x1 * cos - x2 * sin, x2 * cos + x1 * sin], -1)
    return jnp.concatenate([rot.astype(x.dtype), x[..., ROT_DIM:]], -1)


def pool_mixer(a, buf, pos0, w, scale):
    n, t_new, _ = a.shape
    ext = a if buf is None else jnp.concatenate([buf.astype(a.dtype), a], 1)
    lb = ext.shape[1] - t_new
    maxw = max(POOL_WINDOWS)
    c = jnp.cumsum(jnp.pad(ext.astype(jnp.float32), ((0, 0), (maxw + 1, 0), (0, 0))), axis=1)
    e0 = maxw + lb + 1
    end = c[:, e0:e0 + t_new]
    pos = (pos0 + jnp.arange(t_new)).astype(jnp.float32)
    af = a.astype(jnp.float32)
    parts = []
    for gi, win in enumerate(POOL_WINDOWS):
        sl = slice(gi * POOL_GROUP_W, (gi + 1) * POOL_GROUP_W)
        total = end[..., sl] - c[:, e0 - win:e0 - win + t_new, sl]
        cnt = jnp.minimum(float(win), pos + 1.0)[None, :, None]
        parts.append(total / cnt - af[..., sl])
    d = jnp.stack(parts, axis=2)
    y = jnp.einsum('ntgc,gcd->ntgd', d, w.astype(jnp.float32)).reshape(n, t_new, POOL_WIDTH)
    y = y * scale.astype(jnp.float32)
    return y.astype(a.dtype), ext[:, -POOL_KEEP:]


def gmlp_mixer(u, v, ln_g, ln_b, ws, bs):
    n, t, _ = u.shape
    cl = min(CHUNK, t)
    nch = t // cl
    vn = layer_norm(v, ln_g, ln_b)
    causal = jnp.tril(jnp.ones((cl, cl), dtype=bool))
    wm = jnp.where(causal[None], ws[:, :cl, :cl], 0.0).astype(vn.dtype)
    vr = vn.reshape(n, nch, cl, GMLP_GROUPS, GMLP_GROUP_W)
    mix = jnp.einsum('gts,ncsgd->nctgd', wm, vr) + bs[:, :cl].T[None, None, :, :, None].astype(vn.dtype)
    return u * mix.reshape(n, t, GMLP_WIDTH), vn


def compress_blocks(x, pe, w1, b1, w2):
    n, l, h, d = x.shape
    nc = (l - CMP_LEN) // CMP_STRIDE + 1
    idx = np.arange(nc)[:, None] * CMP_STRIDE + np.arange(CMP_LEN)[None, :]
    blk = x[:, idx] + pe[None, None, :, None, :]
    blk = jnp.swapaxes(blk, 2, 3).reshape(n, nc, h, CMP_LEN * d)
    hid = jax.nn.gelu(blk @ w1 + b1)
    return hid @ w2


def selection_overlap(nc, nsb):
    i = np.arange(nc)[:, None]
    j = np.arange(nsb)[None, :]
    lo = np.maximum(i * CMP_STRIDE, j * SEL_LEN)
    hi = np.minimum(i * CMP_STRIDE + CMP_LEN, (j + 1) * SEL_LEN)
    return (np.maximum(hi - lo, 0) // CMP_STRIDE).astype(np.float32)


def sel_attention(qg, ks, vs, idx, qpos):
    n, l, h, d = ks.shape
    tq = qg.shape[1]
    nsb = -(-l // SEL_LEN)
    pad = nsb * SEL_LEN - l
    kb = jnp.pad(ks, ((0, 0), (0, pad), (0, 0), (0, 0))).reshape(n, nsb, SEL_LEN, h, d).transpose(0, 3, 1, 2, 4)
    vb = jnp.pad(vs, ((0, 0), (0, pad), (0, 0), (0, 0))).reshape(n, nsb, SEL_LEN, h, d).transpose(0, 3, 1, 2, 4)
    qb = min(Q_BLOCK, tq)
    nqb = tq // qb
    n_sel = idx.shape[-1]
    q_blocks = jnp.swapaxes(qg.reshape(n, nqb, qb, h, GQA, d), 0, 1)
    idx_blocks = idx.reshape(n, h, nqb, qb, n_sel).transpose(2, 0, 1, 3, 4)
    pos_blocks = qpos.reshape(nqb, qb)
    gather = jax.vmap(jax.vmap(lambda tbl, ix: tbl[ix]))

    def one_block(args):
        qq, ix, pp = args
        kg = gather(kb, ix)
        vg = gather(vb, ix)
        kpos = ix[..., None] * SEL_LEN + jnp.arange(SEL_LEN)
        mask = (kpos <= pp[None, None, :, None, None]).reshape(n, h, 1, qb, n_sel * SEL_LEN)
        s = jnp.einsum('nqhgd,nhqjsd->nhgqjs', qq, kg).reshape(n, h, GQA, qb, n_sel * SEL_LEN)
        p = masked_softmax(s, mask)
        return jnp.einsum('nhgqk,nhqkd->nqhgd', p.astype(vg.dtype), vg.reshape(n, h, qb, n_sel * SEL_LEN, d))

    o = lax.map(one_block, (q_blocks, idx_blocks, pos_blocks))
    return jnp.swapaxes(o, 0, 1).reshape(n, tq, h, GQA, d)


def window_banded(qg, kw, vw):
    n, t, h, g, d = qg.shape
    nqb = t // Q_BLOCK
    nw = WINDOW // Q_BLOCK
    kp = jnp.pad(kw, ((0, 0), (WINDOW, 0), (0, 0), (0, 0))).reshape(n, nqb + nw, Q_BLOCK, h, d)
    vp = jnp.pad(vw, ((0, 0), (WINDOW, 0), (0, 0), (0, 0))).reshape(n, nqb + nw, Q_BLOCK, h, d)
    nk = (nw + 1) * Q_BLOCK
    kband = jnp.stack([kp[:, i:i + nqb] for i in range(nw + 1)], axis=2).reshape(n, nqb, nk, h, d)
    vband = jnp.stack([vp[:, i:i + nqb] for i in range(nw + 1)], axis=2).reshape(n, nqb, nk, h, d)
    qb = qg.reshape(n, nqb, Q_BLOCK, h, g, d)
    s = jnp.einsum('nbqhgd,nbkhd->nhgbqk', qb, kband)
    qpos = jnp.arange(t).reshape(nqb, Q_BLOCK)
    kpos = jnp.arange(nqb)[:, None] * Q_BLOCK - WINDOW + jnp.arange(nk)[None, :]
    mask = (kpos[:, None, :] <= qpos[:, :, None]) & (kpos[:, None, :] > qpos[:, :, None] - WINDOW) & (kpos[:, None, :] >= 0)
    p = masked_softmax(s, mask[None, None, None])
    o = jnp.einsum('nhgbqk,nbkhd->nbqhgd', p.astype(vband.dtype), vband)
    return o.reshape(n, t, h, g, d)


def window_dense(qg, kw, vw, pos0):
    tq = qg.shape[1]
    lw = kw.shape[1]
    kpos = pos0 + tq - lw + jnp.arange(lw)
    qpos = pos0 + jnp.arange(tq)
    mask = (kpos[None, :] <= qpos[:, None]) & (kpos[None, :] > qpos[:, None] - WINDOW)
    s = jnp.einsum('nthgd,nkhd->nhgtk', qg, kw)
    p = masked_softmax(s, mask)
    return jnp.einsum('nhgtk,nkhd->nthgd', p.astype(vw.dtype), vw)


def nsa_mixer(q, kv_cmp, kv_sel, kv_win, gate_logits, pos0, pe, w1, b1, w2):
    n, tq = q.shape[:2]
    qpos = pos0 + jnp.arange(tq)
    qg = (q * (HEAD_DIM ** -0.5)).reshape(n, tq, N_KV_HEADS, GQA, HEAD_DIM)
    kc = compress_blocks(kv_cmp[:, :, 0], pe[0], w1[0], b1[0], w2[0])
    vc = compress_blocks(kv_cmp[:, :, 1], pe[1], w1[1], b1[1], w2[1])
    nc = kc.shape[1]
    cend = jnp.arange(nc) * CMP_STRIDE + CMP_LEN - 1
    cmask = cend[None, :] <= qpos[:, None]
    p_cmp = masked_softmax(jnp.einsum('nthgd,nchd->nhgtc', qg, kc), cmask)
    o_cmp = jnp.einsum('nhgtc,nchd->nthgd', p_cmp.astype(vc.dtype), vc)
    l = kv_sel.shape[1]
    nsb = -(-l // SEL_LEN)
    imp = jnp.einsum('nhgtc,cj->nhtj', p_cmp, jnp.asarray(selection_overlap(nc, nsb)))
    blk = jnp.arange(nsb)[None, :]
    cur = (qpos // SEL_LEN)[:, None]
    forced = (blk == 0) | (blk == cur) | (blk == cur - 1)
    score = jnp.where(forced, FORCE, jnp.where(blk <= cur, imp, -FORCE))
    _, idx = lax.top_k(score, min(SEL_TOPN, nsb))
    o_sel = sel_attention(qg, kv_sel[:, :, 0], kv_sel[:, :, 1], idx, qpos)
    if pos0 == 0 and kv_win.shape[1] == tq and tq % Q_BLOCK == 0:
        o_win = window_banded(qg, kv_win[:, :, 0], kv_win[:, :, 1])
    else:
        o_win = window_dense(qg, kv_win[:, :, 0], kv_win[:, :, 1], pos0)
    g = jax.nn.sigmoid(gate_logits.astype(jnp.float32)).reshape(n, tq, N_KV_HEADS, GQA, 3)
    o = g[..., 0:1] * o_cmp + g[..., 1:2] * o_sel + g[..., 2:3] * o_win
    return o.reshape(n, tq, NSA_WIDTH).astype(q.dtype)


def token_mixers(h, pos0, past, lw):
    n, t, _ = h.shape
    z = h @ lw['w_in']
    a, u, v, q, kv, nsa_g, merge_g = jnp.split(z, split_points(), axis=-1)
    pos = pos0 + jnp.arange(t)
    q = partial_rope(q.reshape(n, t, N_HEADS, HEAD_DIM), pos)
    kv = kv.reshape(n, t, 3, 2, N_KV_HEADS, HEAD_DIM)
    k = partial_rope(kv[:, :, :, 0].reshape(n, t, 3 * N_KV_HEADS, HEAD_DIM), pos).reshape(n, t, 3, N_KV_HEADS, HEAD_DIM)
    kv = jnp.stack([k, kv[:, :, :, 1]], axis=3)
    new_cmp, new_sel, new_win = kv[:, :, 0], kv[:, :, 1], kv[:, :, 2]
    if past is None:
        cmp_all, sel_all, win_all = new_cmp, new_sel, new_win
        keep = min(WINDOW, t)
        pool_buf = None
    else:
        cmp_all = jnp.concatenate([past['cmp'].astype(new_cmp.dtype), new_cmp], 1)
        sel_all = jnp.concatenate([past['sel'].astype(new_sel.dtype), new_sel], 1)
        win_all = jnp.concatenate([past['win'].astype(new_win.dtype), new_win], 1)
        keep = past['win'].shape[1]
        pool_buf = past['pool']
    o_nsa = nsa_mixer(q, cmp_all, sel_all, win_all, nsa_g, pos0, lw['cmp_pe'], lw['cmp_w1'], lw['cmp_b1'], lw['cmp_w2'])
    o_pool, pool_state = pool_mixer(a, pool_buf, pos0, lw['pool_w'], lw['pool_scale'])
    o_gmlp, v_rows = gmlp_mixer(u, v, lw['gmlp_ln_g'], lw['gmlp_ln_b'], lw['gmlp_ws'], lw['gmlp_bs'])
    gates = jax.nn.sigmoid(merge_g.astype(jnp.float32)).reshape(n, t, 3, D_MODEL)
    merged = (gates[:, :, 0] * (o_pool @ lw['w_branch_pool'])
              + gates[:, :, 1] * (o_gmlp @ lw['w_branch_gmlp'])
              + gates[:, :, 2] * (o_nsa @ lw['w_branch_nsa']))
    out = merged.astype(h.dtype) @ lw['w_out']
    return out, (new_cmp, new_sel, win_all[:, -keep:], pool_state, v_rows)


def moe_ffn(h, w_router, b_router, w_gate, w_up, w_down):
    n, t, _ = h.shape
    aff = jax.nn.sigmoid(jnp.einsum('ntd,de->nte', h.astype(jnp.float32), w_router.astype(jnp.float32)))
    score = aff + b_router.astype(jnp.float32)
    grp_top = lax.top_k(score.reshape(n, t, N_EXPERT_GROUPS, EXPERTS_PER_GROUP), TOP_K)[0]
    best = jnp.argmax(jnp.sum(grp_top, -1), -1)
    in_grp = (jnp.arange(N_EXPERTS) // EXPERTS_PER_GROUP)[None, None, :] == best[..., None]
    _, top_idx = lax.top_k(jnp.where(in_grp, score, NEG_INF), TOP_K)
    top_w = jnp.take_along_axis(aff, top_idx, -1)
    top_w = top_w / jnp.sum(top_w, -1, keepdims=True)
    combine = jnp.sum(jax.nn.one_hot(top_idx, N_EXPERTS, dtype=jnp.float32) * top_w[..., None], axis=-2)
    hg = jnp.einsum('ntd,edf->ntef', h, w_gate)
    hu = jnp.einsum('ntd,edf->ntef', h, w_up)
    act = jax.nn.silu(hg) * hu * combine[..., None].astype(h.dtype)
    return jnp.einsum('ntef,efd->ntd', act, w_down)


def trunk_layer(x, pos0, past, lw, w_router, b_router):
    m, st = token_mixers(x, pos0, past, lw)
    x = layer_norm(DN_ALPHA * x + m, lw['ln1_g'], lw['ln1_b'])
    f = moe_ffn(x, w_router, b_router, lw['w_gate'], lw['w_up'], lw['w_down'])
    x = layer_norm(DN_ALPHA * x + f, lw['ln2_g'], lw['ln2_b'])
    return x, st


def gather_pages(pool, page_table):
    g = pool[page_table]
    return g.reshape(g.shape[0], g.shape[1] * g.shape[2], *pool.shape[1:][1:])


def setup_inputs(seed: int = 0) -> dict:
    key = jax.random.key(seed)
    ks = jax.random.split(key, 32)
    f32 = jnp.float32

    def nrm(k, shape, scale):
        return jax.random.normal(k, shape, f32) * scale

    n_pages = PAST_LEN // PAGE_SIZE
    n_used = DEC_BATCH * n_pages
    n_pool = n_used + n_used // 4
    win_keep = min(WINDOW, PAST_LEN)
    page_table = jax.random.permutation(ks[4], n_pool)[:n_used].reshape(DEC_BATCH, n_pages).astype(jnp.int32)
    return {
        'x_prompt': nrm(ks[0], (BATCH, SEQ, D_MODEL), 1.0),
        'x_sample': nrm(ks[1], (DEC_BATCH, DEC_SEQ, D_MODEL), 1.0),
        'cache_cmp_kv': nrm(ks[2], (DEPTH, n_pool, PAGE_SIZE, 2, N_KV_HEADS, HEAD_DIM), 1.0),
        'cache_sel_kv': nrm(ks[3], (DEPTH, n_pool, PAGE_SIZE, 2, N_KV_HEADS, HEAD_DIM), 1.0),
        'state_win_kv': nrm(ks[5], (DEPTH, DEC_BATCH, win_keep, 2, N_KV_HEADS, HEAD_DIM), 1.0),
        'state_pool': nrm(ks[6], (DEPTH, DEC_BATCH, POOL_KEEP, POOL_WIDTH), 1.0),
        'page_table': page_table,
        'w_in': nrm(ks[7], (DEPTH, D_MODEL, IN_WIDTH), D_MODEL ** -0.5),
        'pool_w': nrm(ks[8], (DEPTH, len(POOL_WINDOWS), POOL_GROUP_W, POOL_GROUP_W), POOL_GROUP_W ** -0.5),
        'pool_scale': 1.0 + nrm(ks[9], (DEPTH, POOL_WIDTH), 0.01),
        'gmlp_ln_g': 1.0 + nrm(ks[10], (DEPTH, GMLP_WIDTH), 0.01),
        'gmlp_ln_b': nrm(ks[11], (DEPTH, GMLP_WIDTH), 0.01),
        'gmlp_ws': nrm(ks[12], (DEPTH, GMLP_GROUPS, CHUNK, CHUNK), CHUNK ** -0.5),
        'gmlp_bs': 1.0 + nrm(ks[13], (DEPTH, GMLP_GROUPS, CHUNK), 0.01),
        'cmp_pe': nrm(ks[14], (DEPTH, 2, CMP_LEN, HEAD_DIM), 0.02),
        'cmp_w1': nrm(ks[15], (DEPTH, 2, CMP_LEN * HEAD_DIM, CMP_HIDDEN), (CMP_LEN * HEAD_DIM) ** -0.5),
        'cmp_b1': nrm(ks[16], (DEPTH, 2, CMP_HIDDEN), 0.01),
        'cmp_w2': nrm(ks[17], (DEPTH, 2, CMP_HIDDEN, HEAD_DIM), CMP_HIDDEN ** -0.5),
        'w_branch_pool': nrm(ks[18], (DEPTH, POOL_WIDTH, D_MODEL), POOL_WIDTH ** -0.5),
        'w_branch_gmlp': nrm(ks[19], (DEPTH, GMLP_WIDTH, D_MODEL), GMLP_WIDTH ** -0.5),
        'w_branch_nsa': nrm(ks[20], (DEPTH, NSA_WIDTH, D_MODEL), NSA_WIDTH ** -0.5),
        'w_out': nrm(ks[21], (DEPTH, D_MODEL, D_MODEL), D_MODEL ** -0.5 * DN_BETA),
        'ln1_g': 1.0 + nrm(ks[22], (DEPTH, D_MODEL), 0.01),
        'ln1_b': nrm(ks[23], (DEPTH, D_MODEL), 0.01),
        'ln2_g': 1.0 + nrm(ks[24], (DEPTH, D_MODEL), 0.01),
        'ln2_b': nrm(ks[25], (DEPTH, D_MODEL), 0.01),
        'w_router': nrm(ks[26], (D_MODEL, N_EXPERTS), D_MODEL ** -0.5),
        'b_router': nrm(ks[27], (N_EXPERTS,), 0.01),
        'w_gate': nrm(ks[28], (DEPTH, N_EXPERTS, D_MODEL, D_EXPERT), D_MODEL ** -0.5),
        'w_up': nrm(ks[29], (DEPTH, N_EXPERTS, D_MODEL, D_EXPERT), D_MODEL ** -0.5),
        'w_down': nrm(ks[30], (DEPTH, N_EXPERTS, D_EXPERT, D_MODEL), D_EXPERT ** -0.5 * DN_BETA),
    }


def reference(x_prompt, x_sample, cache_cmp_kv, cache_sel_kv, state_win_kv, state_pool, page_table,
              w_in, pool_w, pool_scale, gmlp_ln_g, gmlp_ln_b, gmlp_ws, gmlp_bs,
              cmp_pe, cmp_w1, cmp_b1, cmp_w2, w_branch_pool, w_branch_gmlp, w_branch_nsa, w_out,
              ln1_g, ln1_b, ln2_g, ln2_b, w_router, b_router, w_gate, w_up, w_down):
    past_len = page_table.shape[1] * cache_cmp_kv.shape[2]
    xp, xs = x_prompt, x_sample
    st_p, st_s = [], []
    for l in range(DEPTH):
        lw = dict(w_in=w_in[l], pool_w=pool_w[l], pool_scale=pool_scale[l],
                  gmlp_ln_g=gmlp_ln_g[l], gmlp_ln_b=gmlp_ln_b[l], gmlp_ws=gmlp_ws[l], gmlp_bs=gmlp_bs[l],
                  cmp_pe=cmp_pe[l], cmp_w1=cmp_w1[l], cmp_b1=cmp_b1[l], cmp_w2=cmp_w2[l],
                  w_branch_pool=w_branch_pool[l], w_branch_gmlp=w_branch_gmlp[l], w_branch_nsa=w_branch_nsa[l],
                  w_out=w_out[l], ln1_g=ln1_g[l], ln1_b=ln1_b[l], ln2_g=ln2_g[l], ln2_b=ln2_b[l],
                  w_gate=w_gate[l], w_up=w_up[l], w_down=w_down[l])
        past = dict(cmp=gather_pages(cache_cmp_kv[l], page_table),
                    sel=gather_pages(cache_sel_kv[l], page_table),
                    win=state_win_kv[l], pool=state_pool[l])
        xp, sp = trunk_layer(xp, 0, None, lw, w_router, b_router)
        xs, ss = trunk_layer(xs, past_len, past, lw, w_router, b_router)
        st_p.append(sp)
        st_s.append(ss)

    def stacked(states, i):
        return jnp.stack([s[i] for s in states], 0)

    return (xp, xs,
            stacked(st_p, 0), stacked(st_p, 1), stacked(st_p, 2), stacked(st_p, 3),
            stacked(st_s, 0), stacked(st_s, 1), stacked(st_s, 2), stacked(st_s, 3), stacked(st_s, 4))
```

```python
import functools
import math

import numpy as np
import jax
import jax.numpy as jnp
from jax import lax
from jax.experimental import pallas as pl
from jax.experimental.pallas import tpu as pltpu

F32 = jnp.float32
BF16 = jnp.bfloat16

POOL_WINDOWS = (2, 4, 8, 16)
POOL_KEEP = max(POOL_WINDOWS) - 1
GMLP_GROUPS = 4
CHUNK = 128
N_HEADS = 8
HEAD_DIM = 64
N_KV_HEADS = 2
GQA = N_HEADS // N_KV_HEADS
ROT_DIM = HEAD_DIM // 4
ROPE_THETA = 500000.0
CMP_LEN = 32
CMP_STRIDE = 16
CMP_HIDDEN = 2 * HEAD_DIM
SEL_LEN = 64
SEL_TOPN = 16
WINDOW = 512
Q_BLOCK = 128
N_EXPERTS = 16
EXPERTS_PER_GROUP = 4
LN_EPS = 1e-5
NEG_INF = -1e30
FORCE = 1e9

LANES = 128
KEY_CHUNK = 256
VMEM_LIMIT = 56 * 1024 * 1024


def _cparams(*sem):
    return pltpu.CompilerParams(dimension_semantics=sem, vmem_limit_bytes=VMEM_LIMIT)


def _tile(n, cap):
    t = cap
    while n % t:
        t //= 2
    return t


def _dot(a, b):
    return jnp.dot(a, b, preferred_element_type=F32)


def _dot_nt(a, b):
    return lax.dot_general(a, b, (((1,), (1,)), ((), ())), preferred_element_type=F32)


def _layer_norm(x, g, b):
    mu = jnp.mean(x, -1, keepdims=True)
    xc = x - mu
    var = jnp.mean(xc * xc, -1, keepdims=True)
    return xc * lax.rsqrt(var + LN_EPS) * g + b


def _masked_softmax(s, mask):
    sm = jnp.where(mask, s, NEG_INF)
    p = jnp.where(mask, jnp.exp(sm - jnp.max(sm, -1, keepdims=True)), 0.0)
    d = jnp.sum(p, -1, keepdims=True)
    return p / jnp.where(d > 0, d, 1.0)


def _chunked_attention(q, kt_ref, v_ref, mask_fn, c_lo, c_hi):
    rows = q.shape[0]

    def step(c, carry):
        m, l, acc = carry
        mask = mask_fn(c)
        sm = jnp.where(mask, _dot(q, kt_ref[c]), NEG_INF)
        m_new = jnp.maximum(m, jnp.max(sm, -1, keepdims=True))
        alpha = jnp.exp(m - m_new)
        p = jnp.where(mask, jnp.exp(sm - m_new), 0.0)
        l = alpha * l + jnp.sum(p, -1, keepdims=True)
        acc = alpha * acc + _dot(p.astype(BF16), v_ref[c])
        return m_new, l, acc

    init = (jnp.full((rows, 1), NEG_INF, F32), jnp.zeros((rows, 1), F32), jnp.zeros((rows, v_ref.shape[-1]), F32))
    _, l, acc = lax.fori_loop(c_lo, c_hi, step, init)
    return acc / jnp.where(l > 0, l, 1.0)


def _topk_mask(score, k):
    lane = lax.broadcasted_iota(jnp.int32, score.shape, 1).astype(F32)

    def body(_, carry):
        sc, sel = carry
        m = jnp.max(sc, -1, keepdims=True)
        first = jnp.min(jnp.where(sc == m, lane, float(2 * LANES)), -1, keepdims=True)
        pick = lane == first
        return jnp.where(pick, -jnp.inf, sc), jnp.where(pick, 1.0, sel)

    _, sel = lax.fori_loop(0, k, body, (score, jnp.zeros_like(score)), unroll=True)
    return sel


def _importance(p_cmp, rows, ovl):
    psum = p_cmp[0:rows]
    for g in range(1, GQA):
        psum = psum + p_cmp[g * rows:(g + 1) * rows]
    hi = psum.astype(BF16)
    lo = (psum - hi.astype(F32)).astype(BF16)
    return _dot(hi, ovl) + _dot(lo, ovl)


def _select_blocks(imp, qpos, nsb):
    blk = lax.broadcasted_iota(jnp.int32, imp.shape, 1)
    cur = jnp.right_shift(qpos, int(math.log2(SEL_LEN)))
    forced = (blk == 0) | (blk == cur) | (blk == cur - 1)
    score = jnp.where(forced, FORCE, jnp.where(blk <= cur, imp, -FORCE))
    score = jnp.where(blk < nsb, score, -jnp.inf)
    return _topk_mask(score, min(SEL_TOPN, nsb))


def _head_gate(gs, col):
    lane = lax.broadcasted_iota(jnp.int32, gs.shape, 1)
    return jnp.sum(jnp.where(lane == col, gs, 0.0), -1, keepdims=True)


def _proj_kernel(x_ref, w_ref, wg_ref, cos_ref, sm_ref, sp_ref, auv_ref, q_ref, kv_ref, g_ref, *, n_auv, n_q, n_kv):
    xb = x_ref[...].astype(BF16)
    z = _dot(xb, w_ref[...])
    auv_ref[...] = z[:, :n_auv]
    cos, sm, sp = cos_ref[...], sm_ref[...], sp_ref[...]
    half = ROT_DIM // 2

    def rope(t):
        return t * cos + pltpu.roll(t, LANES - half, 1) * sm + pltpu.roll(t, half, 1) * sp

    for j in range(n_q // LANES):
        c0 = n_auv + j * LANES
        q_ref[:, j * LANES:(j + 1) * LANES] = rope(z[:, c0:c0 + LANES]) * (HEAD_DIM ** -0.5)
    for j in range(n_kv // LANES):
        c0 = n_auv + n_q + j * LANES
        t = z[:, c0:c0 + LANES]
        kv_ref[:, j * LANES:(j + 1) * LANES] = rope(t) if j % 2 == 0 else t
    g_ref[...] = _dot(xb, wg_ref[...])


def _proj(x, w_main, w_gate, cos, sm, sp, n_auv, n_q, n_kv):
    n, d = x.shape
    tm = _tile(n, 512)
    row = lambda i: (i, 0)
    fixed = lambda i: (0, 0)
    return pl.pallas_call(
        functools.partial(_proj_kernel, n_auv=n_auv, n_q=n_q, n_kv=n_kv),
        grid=(n // tm,),
        in_specs=[pl.BlockSpec((tm, d), row),
                  pl.BlockSpec(w_main.shape, fixed),
                  pl.BlockSpec(w_gate.shape, fixed),
                  pl.BlockSpec((tm, LANES), row), pl.BlockSpec((tm, LANES), row), pl.BlockSpec((tm, LANES), row)],
        out_specs=[pl.BlockSpec((tm, n_auv), row), pl.BlockSpec((tm, n_q), row),
                   pl.BlockSpec((tm, n_kv), row), pl.BlockSpec((tm, LANES), row)],
        out_shape=[jax.ShapeDtypeStruct((n, n_auv), F32), jax.ShapeDtypeStruct((n, n_q), F32),
                   jax.ShapeDtypeStruct((n, n_kv), F32), jax.ShapeDtypeStruct((n, LANES), F32)],
        compiler_params=_cparams("parallel"),
        name="in_proj",
    )(x, w_main, w_gate, cos, sm, sp)


def _pool_select(grp, vals):
    out = vals[-1]
    for gi in range(len(vals) - 2, -1, -1):
        out = jnp.where(grp == gi, vals[gi], out)
    return out


def _mix_prompt_kernel(auv_ref, halo_ref, pw_ref, ps_ref, lng_ref, lnb_ref, ws_ref, bsr_ref, op_ref, og_ref, *, tm, pw):
    i = pl.program_id(1)
    halo_rows = halo_ref.shape[0]
    a = auv_ref[:, 0:pw]
    u = auv_ref[:, pw:2 * pw]
    v = auv_ref[:, 2 * pw:3 * pw]
    halo = jnp.where(i == 0, 0.0, halo_ref[...])
    ext = jnp.concatenate([halo, a], axis=0)
    sums, cur, width = [], ext, 1
    for win in POOL_WINDOWS:
        while width < win:
            cur = cur + pltpu.roll(cur, width, 0)
            width *= 2
        sums.append(cur[halo_rows:])
    lane = lax.broadcasted_iota(jnp.int32, (tm, pw), 1)
    grp = lane // (pw // len(POOL_WINDOWS))
    pos = (i * tm + lax.broadcasted_iota(jnp.int32, (tm, pw), 0)).astype(F32)
    total = _pool_select(grp, sums)
    win = _pool_select(grp, [jnp.full((tm, pw), float(w), F32) for w in POOL_WINDOWS])
    d = total / jnp.minimum(win, pos + 1.0) - a
    op_ref[...] = (_dot(d.astype(BF16), pw_ref[...]) * ps_ref[...]).astype(op_ref.dtype)

    vn = _layer_norm(v, lng_ref[...], lnb_ref[...]).astype(BF16)
    r = lax.broadcasted_iota(jnp.int32, (CHUNK, CHUNK), 0)
    c = lax.broadcasted_iota(jnp.int32, (CHUNK, CHUNK), 1)
    wms = [jnp.where(c <= r, ws_ref[g], 0.0).astype(BF16) for g in range(GMLP_GROUPS)]
    grp_c = lax.broadcasted_iota(jnp.int32, (CHUNK, pw), 1) // (pw // GMLP_GROUPS)
    for j in range(tm // CHUNK):
        rows = slice(j * CHUNK, (j + 1) * CHUNK)
        vc = vn[rows]
        mix = _pool_select(grp_c, [_dot(wms[g], vc) for g in range(GMLP_GROUPS)])
        og_ref[rows, :] = (u[rows] * (mix + bsr_ref[...])).astype(og_ref.dtype)


def _mix_prompt(z_auv, batch, seq, pool_bd, pool_scale, ln_g, ln_b, ws, bs_rows):
    pw = pool_scale.shape[-1]
    tm = _tile(seq, 512)
    nt = seq // tm
    halo = 2 * 8
    hb = tm // halo
    fixed2 = lambda b, i: (0, 0)
    return pl.pallas_call(
        functools.partial(_mix_prompt_kernel, tm=tm, pw=pw),
        grid=(batch, nt),
        in_specs=[pl.BlockSpec((tm, 3 * pw), lambda b, i: (b * nt + i, 0)),
                  pl.BlockSpec((halo, pw), lambda b, i: (jnp.maximum((b * nt + i) * hb - 1, 0), 0)),
                  pl.BlockSpec((pw, pw), fixed2), pl.BlockSpec((1, pw), fixed2),
                  pl.BlockSpec((1, pw), fixed2), pl.BlockSpec((1, pw), fixed2),
                  pl.BlockSpec(ws.shape, lambda b, i: (0, 0, 0)),
                  pl.BlockSpec((CHUNK, pw), fixed2)],
        out_specs=[pl.BlockSpec((tm, pw), lambda b, i: (b * nt + i, 0)),
                   pl.BlockSpec((tm, pw), lambda b, i: (b * nt + i, 0))],
        out_shape=[jax.ShapeDtypeStruct((batch * seq, pw), BF16), jax.ShapeDtypeStruct((batch * seq, pw), BF16)],
        compiler_params=_cparams("parallel", "parallel"),
        name="mix_prompt",
    )(z_auv, z_auv, pool_bd, pool_scale, ln_g, ln_b, ws, bs_rows)


def _mix_sample_kernel(a_ref, st_ref, pw_ref, ps_ref, lng_ref, lnb_ref, wrow_ref, bsr_ref, op_ref, og_ref, vn_ref,
                       *, tq, keep, pos0, pw):
    nb = a_ref.shape[1]
    ext = [st_ref[i] for i in range(keep)] + [a_ref[t][:, 0:pw] for t in range(tq)]
    grp = lax.broadcasted_iota(jnp.int32, (nb, pw), 1) // (pw // len(POOL_WINDOWS))
    for t in range(tq):
        e = keep + t
        acc, k, sums = ext[e], 1, []
        for win in POOL_WINDOWS:
            while k < win:
                if e - k >= 0:
                    acc = acc + ext[e - k]
                k += 1
            sums.append(acc)
        total = _pool_select(grp, sums)
        cnt = _pool_select(grp, [jnp.full((nb, pw), min(float(w), pos0 + t + 1.0), F32) for w in POOL_WINDOWS])
        d = total / cnt - ext[e]
        op_ref[t] = (_dot(d.astype(BF16), pw_ref[...]) * ps_ref[...]).astype(op_ref.dtype)
    vns = []
    for t in range(tq):
        vn = _layer_norm(a_ref[t][:, 2 * pw:3 * pw], lng_ref[...], lnb_ref[...])
        vn_ref[t] = vn
        vns.append(vn)
    for t in range(tq):
        mix = bsr_ref[t:t + 1, :]
        for s in range(t + 1):
            mix = mix + wrow_ref[t * tq + s:t * tq + s + 1, :] * vns[s]
        og_ref[t] = (a_ref[t][:, pw:2 * pw] * mix).astype(og_ref.dtype)


def _mix_sample(a_t, st_t, pos0, pool_bd, pool_scale, ln_g, ln_b, wrow, bsrow):
    tq, nb, w3 = a_t.shape
    keep = st_t.shape[0]
    pw = w3 // 3
    full = lambda shape: pl.BlockSpec(shape, lambda i: (0,) * len(shape))
    return pl.pallas_call(
        functools.partial(_mix_sample_kernel, tq=tq, keep=keep, pos0=pos0, pw=pw),
        grid=(1,),
        in_specs=[full(a_t.shape), full(st_t.shape), full(pool_bd.shape), full(pool_scale.shape),
                  full(ln_g.shape), full(ln_b.shape), full(wrow.shape), full(bsrow.shape)],
        out_specs=[full((tq, nb, pw)), full((tq, nb, pw)), full((tq, nb, pw))],
        out_shape=[jax.ShapeDtypeStruct((tq, nb, pw), BF16), jax.ShapeDtypeStruct((tq, nb, pw), BF16),
                   jax.ShapeDtypeStruct((tq, nb, pw), F32)],
        compiler_params=_cparams("arbitrary"),
        name="mix_sample",
    )(a_t, st_t, pool_bd, pool_scale, ln_g, ln_b, wrow, bsrow)


def _cmp1_kernel(x_ref, w_ref, pe_ref, h_ref, *, g):
    hw = N_KV_HEADS * CMP_HIDDEN
    xcat = jnp.concatenate([x_ref[pl.ds(r, g, stride=CMP_STRIDE), :] for r in range(CMP_STRIDE)], axis=1)
    for ab in range(2):
        lhs = (xcat + pe_ref[ab]).astype(BF16)
        h_ref[:, ab * hw:(ab + 1) * hw] = _dot(lhs, w_ref[ab])


def _cmp1(x2d, row_block0, n_rows, w1_bd, pe_rows):
    tr = _tile(n_rows, 4096)
    g = tr // CMP_STRIDE
    kvw = N_KV_HEADS * HEAD_DIM
    hcols = 2 * N_KV_HEADS * CMP_HIDDEN
    return pl.pallas_call(
        functools.partial(_cmp1_kernel, g=g),
        grid=(n_rows // tr, 2),
        in_specs=[pl.BlockSpec((tr, kvw), lambda i, kv: (row_block0(tr) + i, kv)),
                  pl.BlockSpec((None,) + w1_bd.shape[1:], lambda i, kv: (kv, 0, 0, 0)),
                  pl.BlockSpec((None,) + pe_rows.shape[1:], lambda i, kv: (kv, 0, 0, 0))],
        out_specs=pl.BlockSpec((g, hcols), lambda i, kv: (i, kv)),
        out_shape=jax.ShapeDtypeStruct((n_rows // CMP_STRIDE, 2 * hcols), F32),
        compiler_params=_cparams("parallel", "parallel"),
        name="cmp_stage1",
    )(x2d, w1_bd, pe_rows)


def _gelu_tanh(x):
    return 0.5 * x * (1.0 + jnp.tanh(math.sqrt(2.0 / math.pi) * (x + 0.044715 * (x * x * x))))


def _cmp2_body(h_refs, b1_ref, w2kt_ref, w2v_ref, kct_ref, vc_ref):
    hall = jnp.concatenate([r[...] for r in h_refs], axis=0) if len(h_refs) > 1 else h_refs[0][...]
    g = hall.shape[0]
    hw = N_KV_HEADS * CMP_HIDDEN
    for kv in range(2):
        for h in range(N_KV_HEADS):
            ca = (kv * 2) * hw + h * CMP_HIDDEN
            cb = (kv * 2 + 1) * hw + h * CMP_HIDDEN
            hid = hall[:, ca:ca + CMP_HIDDEN] + pltpu.roll(hall[:, cb:cb + CMP_HIDDEN], g - 1, 0) + b1_ref[kv]
            hid = _gelu_tanh(hid).astype(BF16)
            if kv == 0:
                kct_ref[h] = _dot_nt(w2kt_ref[...], hid).astype(kct_ref.dtype)
            else:
                vc_ref[h] = _dot(hid, w2v_ref[...]).astype(vc_ref.dtype)


def _cmp2_prompt_kernel(h_ref, b1_ref, w2kt_ref, w2v_ref, kct_ref, vc_ref):
    _cmp2_body([h_ref], b1_ref, w2kt_ref, w2v_ref, kct_ref, vc_ref)


def _cmp2_sample_kernel(pt_ref, *refs, npages):
    _cmp2_body(list(refs[:npages]), *refs[npages:])


def _cmp2_out(n, g):
    specs = [pl.BlockSpec((None, N_KV_HEADS, HEAD_DIM, g), lambda b, *_: (b, 0, 0, 0)),
             pl.BlockSpec((None, N_KV_HEADS, g, HEAD_DIM), lambda b, *_: (b, 0, 0, 0))]
    shapes = [jax.ShapeDtypeStruct((n, N_KV_HEADS, HEAD_DIM, g), BF16),
              jax.ShapeDtypeStruct((n, N_KV_HEADS, g, HEAD_DIM), BF16)]
    return specs, shapes


def _cmp2_prompt(h1, b1, w2kt, w2v):
    n, g, hc = h1.shape
    out_specs, out_shape = _cmp2_out(n, g)
    return pl.pallas_call(
        _cmp2_prompt_kernel,
        grid=(n,),
        in_specs=[pl.BlockSpec((None, g, hc), lambda b: (b, 0, 0)),
                  pl.BlockSpec(b1.shape, lambda b: (0, 0, 0)),
                  pl.BlockSpec(w2kt.shape, lambda b: (0, 0)),
                  pl.BlockSpec(w2v.shape, lambda b: (0, 0))],
        out_specs=out_specs, out_shape=out_shape,
        compiler_params=_cparams("parallel"),
        name="cmp_stage2_prompt",
    )(h1, b1, w2kt, w2v)


def _cmp2_sample(page_table, h1_pool, b1, w2kt, w2v):
    n, npages = page_table.shape
    _, gp, hc = h1_pool.shape
    out_specs, out_shape = _cmp2_out(n, npages * gp)
    page_specs = [pl.BlockSpec((None, gp, hc), functools.partial(lambda b, pt, j: (pt[b, j], 0, 0), j=j))
                  for j in range(npages)]
    return pl.pallas_call(
        functools.partial(_cmp2_sample_kernel, npages=npages),
        grid_spec=pltpu.PrefetchScalarGridSpec(
            num_scalar_prefetch=1, grid=(n,),
            in_specs=page_specs + [pl.BlockSpec(b1.shape, lambda b, pt: (0, 0, 0)),
                                   pl.BlockSpec(w2kt.shape, lambda b, pt: (0, 0)),
                                   pl.BlockSpec(w2v.shape, lambda b, pt: (0, 0))],
            out_specs=out_specs),
        out_shape=out_shape,
        compiler_params=_cparams("parallel"),
        name="cmp_stage2_sample",
    )(page_table, *([h1_pool] * npages), b1, w2kt, w2v)


def _nsa_prompt_kernel(q_ref, g_ref, kct_ref, vc_ref, kts_ref, vs_ref, ktw_ref, vw_ref, ovl_ref, o_ref, *, nsb):
    h = pl.program_id(1)
    qb = pl.program_id(2)
    tq = Q_BLOCK
    rows = GQA * tq
    t0 = qb * tq
    qf = q_ref[...]
    q = jnp.concatenate([qf[:, g * HEAD_DIM:(g + 1) * HEAD_DIM] for g in range(GQA)], axis=0).astype(BF16)
    qpos_r = t0 + (lax.broadcasted_iota(jnp.int32, (rows, 1), 0) & (tq - 1))

    ncp = kct_ref.shape[-1]
    s = _dot(q, kct_ref[...])
    cend = lax.broadcasted_iota(jnp.int32, (1, ncp), 1) * CMP_STRIDE + (CMP_LEN - 1)
    p_cmp = _masked_softmax(s, cend <= qpos_r)
    o_cmp = _dot(p_cmp.astype(BF16), vc_ref[...])

    imp = _importance(p_cmp, tq, ovl_ref[...])
    qpos = t0 + lax.broadcasted_iota(jnp.int32, imp.shape, 0)
    selm = _select_blocks(imp, qpos, nsb).astype(BF16)
    per_chunk = KEY_CHUNK // SEL_LEN
    jrow = lax.broadcasted_iota(jnp.int32, (selm.shape[1], KEY_CHUNK), 0)
    kblk = jnp.right_shift(lax.broadcasted_iota(jnp.int32, (selm.shape[1], KEY_CHUNK), 1), int(math.log2(SEL_LEN)))
    kcol = lax.broadcasted_iota(jnp.int32, (1, KEY_CHUNK), 1)

    def sel_mask(c):
        expand = (jrow == c * per_chunk + kblk).astype(BF16)
        e = _dot(selm, expand)
        e = jnp.concatenate([e] * GQA, axis=0)
        return (e > 0.5) & (c * KEY_CHUNK + kcol <= qpos_r)

    def win_mask(c):
        kpos = c * KEY_CHUNK + kcol
        return (kpos <= qpos_r) & (kpos > qpos_r - WINDOW)

    c_hi = (t0 + tq - 1) // KEY_CHUNK + 1
    c_lo = jnp.maximum(t0 - WINDOW + 1, 0) // KEY_CHUNK
    o_sel = _chunked_attention(q, kts_ref, vs_ref, sel_mask, 0, c_hi)
    o_win = _chunked_attention(q, ktw_ref, vw_ref, win_mask, c_lo, c_hi)

    gs = jax.nn.sigmoid(g_ref[...])
    outs = []
    for g in range(GQA):
        r = slice(g * tq, (g + 1) * tq)
        col = (h * GQA + g) * 3
        outs.append(_head_gate(gs, col) * o_cmp[r] + _head_gate(gs, col + 1) * o_sel[r] + _head_gate(gs, col + 2) * o_win[r])
    o_ref[...] = jnp.concatenate(outs, axis=1).astype(o_ref.dtype)


def _nsa_prompt(z_q, z_g, kct, vc, kts, vs, ktw, vw, ovl, batch, seq):
    nqb = seq // Q_BLOCK
    hq = GQA * HEAD_DIM
    nsb = -(-seq // SEL_LEN)
    per_head = lambda shape: pl.BlockSpec((None, None) + shape, lambda b, h, i: (b, h) + (0,) * len(shape))
    return pl.pallas_call(
        functools.partial(_nsa_prompt_kernel, nsb=nsb),
        grid=(batch, N_KV_HEADS, nqb),
        in_specs=[pl.BlockSpec((Q_BLOCK, hq), lambda b, h, i: (b * nqb + i, h)),
                  pl.BlockSpec((Q_BLOCK, LANES), lambda b, h, i: (b * nqb + i, 0)),
                  per_head(kct.shape[2:]), per_head(vc.shape[2:]),
                  per_head(kts.shape[2:]), per_head(vs.shape[2:]),
                  per_head(ktw.shape[2:]), per_head(vw.shape[2:]),
                  pl.BlockSpec(ovl.shape, lambda b, h, i: (0, 0))],
        out_specs=pl.BlockSpec((Q_BLOCK, hq), lambda b, h, i: (b * nqb + i, h)),
        out_shape=jax.ShapeDtypeStruct((batch * seq, N_KV_HEADS * hq), BF16),
        compiler_params=_cparams("parallel", "parallel", "parallel"),
        name="nsa_prompt",
    )(z_q, z_g, kct, vc, kts, vs, ktw, vw, ovl)


def _nsa_sample_kernel(pt_ref, q_ref, g_ref, kvn_ref, kct_ref, vc_ref, ovl_ref, exp_ref, win_ref, *rest,
                       npages, tq, pos0, nsb):
    page_refs = rest[:npages]
    o_ref = rest[npages]
    newpg = rest[npages + 1]
    page = newpg.shape[0]
    kvw = N_KV_HEADS * HEAD_DIM
    newpg[...] = jnp.zeros(newpg.shape, newpg.dtype)
    newpg[0:tq, :] = kvn_ref[...]
    rows = GQA * tq
    qpos_r = pos0 + lax.rem(lax.broadcasted_iota(jnp.int32, (rows, 1), 0), tq)
    qf = q_ref[...]
    gs = jax.nn.sigmoid(g_ref[...])
    keep = win_ref.shape[0]
    outs = []
    for h in range(N_KV_HEADS):
        q = jnp.concatenate([qf[:, (h * GQA + g) * HEAD_DIM:(h * GQA + g + 1) * HEAD_DIM] for g in range(GQA)],
                            axis=0).astype(BF16)
        kc0 = h * HEAD_DIM
        vc0 = kvw + h * HEAD_DIM

        ncp = kct_ref.shape[-1]
        s = _dot(q, kct_ref[h])
        cend = lax.broadcasted_iota(jnp.int32, (1, ncp), 1) * CMP_STRIDE + (CMP_LEN - 1)
        p_cmp = _masked_softmax(s, cend <= qpos_r)
        o_cmp = _dot(p_cmp.astype(BF16), vc_ref[h])

        imp = _importance(p_cmp, tq, ovl_ref[...])
        qpos = pos0 + lax.broadcasted_iota(jnp.int32, imp.shape, 0)
        selm = _select_blocks(imp, qpos, nsb).astype(BF16)
        e = _dot(jnp.concatenate([selm] * GQA, axis=0), exp_ref[...])
        ks = [page_refs[p][:, kc0:kc0 + HEAD_DIM] for p in range(npages)] + [newpg[:, 2 * kvw + kc0:2 * kvw + kc0 + HEAD_DIM]]
        vs = [page_refs[p][:, vc0:vc0 + HEAD_DIM] for p in range(npages)] + [newpg[:, 2 * kvw + vc0:2 * kvw + vc0 + HEAD_DIM]]
        s = jnp.concatenate([_dot_nt(q, k.astype(BF16)) for k in ks], axis=1)
        kpos = lax.broadcasted_iota(jnp.int32, (1, s.shape[1]), 1)
        p = _masked_softmax(s, (e > 0.5) & (kpos <= qpos_r)).astype(BF16)
        o_sel = _dot(p[:, 0:page], vs[0].astype(BF16))
        for j in range(1, npages + 1):
            o_sel = o_sel + _dot(p[:, j * page:(j + 1) * page], vs[j].astype(BF16))

        nwc = keep // page
        ks = [win_ref[j * page:(j + 1) * page, kc0:kc0 + HEAD_DIM] for j in range(nwc)] + [newpg[:, 4 * kvw + kc0:4 * kvw + kc0 + HEAD_DIM]]
        vs = [win_ref[j * page:(j + 1) * page, vc0:vc0 + HEAD_DIM] for j in range(nwc)] + [newpg[:, 4 * kvw + vc0:4 * kvw + vc0 + HEAD_DIM]]
        s = jnp.concatenate([_dot_nt(q, k.astype(BF16)) for k in ks], axis=1)
        kpos = pos0 - keep + lax.broadcasted_iota(jnp.int32, (1, s.shape[1]), 1)
        p = _masked_softmax(s, (kpos <= qpos_r) & (kpos > qpos_r - WINDOW)).astype(BF16)
        o_win = _dot(p[:, 0:page], vs[0].astype(BF16))
        for j in range(1, nwc + 1):
            o_win = o_win + _dot(p[:, j * page:(j + 1) * page], vs[j].astype(BF16))

        for g in range(GQA):
            r = slice(g * tq, (g + 1) * tq)
            col = (h * GQA + g) * 3
            outs.append(_head_gate(gs, col) * o_cmp[r] + _head_gate(gs, col + 1) * o_sel[r]
                        + _head_gate(gs, col + 2) * o_win[r])
    o_ref[...] = jnp.concatenate(outs, axis=1).astype(o_ref.dtype)


def _nsa_sample(page_table, z_q, z_g, z_kv, row0, kct, vc, ovl, expand, cache_sel, state_win, layer, tq, pos0, nsb):
    n, npages = page_table.shape
    page = cache_sel.shape[2]
    blk0 = row0 // tq
    hq = z_q.shape[1]
    page_specs = [pl.BlockSpec((None, None, page, cache_sel.shape[3]),
                               functools.partial(lambda b, pt, j: (layer, pt[b, j], 0, 0), j=j)) for j in range(npages)]
    return pl.pallas_call(
        functools.partial(_nsa_sample_kernel, npages=npages, tq=tq, pos0=pos0, nsb=nsb),
        grid_spec=pltpu.PrefetchScalarGridSpec(
            num_scalar_prefetch=1, grid=(n,),
            in_specs=[pl.BlockSpec((tq, hq), lambda b, pt: (blk0 + b, 0)),
                      pl.BlockSpec((tq, LANES), lambda b, pt: (blk0 + b, 0)),
                      pl.BlockSpec((tq, z_kv.shape[1]), lambda b, pt: (blk0 + b, 0)),
                      pl.BlockSpec((None,) + kct.shape[1:], lambda b, pt: (b, 0, 0, 0)),
                      pl.BlockSpec((None,) + vc.shape[1:], lambda b, pt: (b, 0, 0, 0)),
                      pl.BlockSpec(ovl.shape, lambda b, pt: (0, 0)),
                      pl.BlockSpec(expand.shape, lambda b, pt: (0, 0)),
                      pl.BlockSpec((None, None) + state_win.shape[2:], lambda b, pt: (layer, b, 0, 0))] + page_specs,
            out_specs=pl.BlockSpec((tq, hq), lambda b, pt: (b, 0)),
            scratch_shapes=[pltpu.VMEM((page, z_kv.shape[1]), F32)]),
        out_shape=jax.ShapeDtypeStruct((n * tq, hq), F32),
        compiler_params=_cparams("parallel"),
        name="nsa_sample",
    )(page_table, z_q, z_g, z_kv, kct, vc, ovl, expand, state_win, *([cache_sel] * npages))


def _route(x1, wrt, br, ct_ref):
    logits = lax.dot_general(wrt, x1, (((1,), (1,)), ((), ())), precision=lax.Precision.HIGHEST,
                             preferred_element_type=F32)
    aff = jax.nn.sigmoid(logits)
    score = aff + br
    sc = [score[e:e + 1, :] for e in range(N_EXPERTS)]
    af = [aff[e:e + 1, :] for e in range(N_EXPERTS)]
    n_grp = N_EXPERTS // EXPERTS_PER_GROUP
    gsum = []
    for k in range(n_grp):
        m = sc[k * EXPERTS_PER_GROUP:(k + 1) * EXPERTS_PER_GROUP]
        best = None
        for i in range(EXPERTS_PER_GROUP):
            for j in range(i + 1, EXPERTS_PER_GROUP):
                pair = m[i] + m[j]
                best = pair if best is None else jnp.maximum(best, pair)
        gsum.append(best)
    is_best, taken = [], None
    for k in range(n_grp):
        ok = None
        for j in range(k + 1, n_grp):
            c = gsum[k] >= gsum[j]
            ok = c if ok is None else ok & c
        if ok is None:
            ok = jnp.ones_like(gsum[k], dtype=jnp.bool_)
        if taken is not None:
            ok = ok & jnp.logical_not(taken)
        is_best.append(ok)
        taken = ok if taken is None else taken | ok
    sel = []
    for e in range(N_EXPERTS):
        k = e // EXPERTS_PER_GROUP
        rank = jnp.zeros_like(sc[e])
        for e2 in range(k * EXPERTS_PER_GROUP, (k + 1) * EXPERTS_PER_GROUP):
            if e2 == e:
                continue
            ahead = (sc[e2] >= sc[e]) if e2 < e else (sc[e2] > sc[e])
            rank = rank + ahead.astype(F32)
        sel.append(is_best[k] & (rank < 2.0))
    denom = jnp.zeros_like(sc[0])
    for e in range(N_EXPERTS):
        denom = denom + jnp.where(sel[e], af[e], 0.0)
    for e in range(N_EXPERTS):
        ct_ref[e:e + 1, :] = jnp.where(sel[e], af[e] / denom, 0.0)


def _merge_kernel(x_ref, op_ref, og_ref, on_ref, wm_ref, wp_ref, wgm_ref, wn_ref, wo_ref, lg_ref, lb_ref,
                  wrt_ref, br_ref, x1_ref, ct_ref, *, alpha):
    x = x_ref[...]
    xb = x.astype(BF16)
    d = x.shape[1]
    merged = None
    for b, (o_r, w_r) in enumerate(((op_ref, wp_ref), (og_ref, wgm_ref), (on_ref, wn_ref))):
        gate = jax.nn.sigmoid(_dot(xb, wm_ref[:, b * d:(b + 1) * d]))
        term = gate * _dot(o_r[...], w_r[...])
        merged = term if merged is None else merged + term
    out = _dot(merged.astype(BF16), wo_ref[...])
    x1 = _layer_norm(alpha * x + out, lg_ref[...], lb_ref[...])
    x1_ref[...] = x1
    _route(x1, wrt_ref[...], br_ref[...], ct_ref)


def _merge(x, o_pool, o_gmlp, o_nsa, w_merge, w_bp, w_bg, w_bn, w_out, ln_g, ln_b, w_router_t, b_router, alpha):
    n, d = x.shape
    tm = _tile(n, 512)
    row = lambda i: (i, 0)
    fixed = lambda a: pl.BlockSpec(a.shape, lambda i: (0, 0))
    return pl.pallas_call(
        functools.partial(_merge_kernel, alpha=alpha),
        grid=(n // tm,),
        in_specs=[pl.BlockSpec((tm, d), row), pl.BlockSpec((tm, o_pool.shape[1]), row),
                  pl.BlockSpec((tm, o_gmlp.shape[1]), row), pl.BlockSpec((tm, o_nsa.shape[1]), row),
                  fixed(w_merge), fixed(w_bp), fixed(w_bg), fixed(w_bn), fixed(w_out), fixed(ln_g), fixed(ln_b),
                  fixed(w_router_t), fixed(b_router)],
        out_specs=[pl.BlockSpec((tm, d), row), pl.BlockSpec((N_EXPERTS, tm), lambda i: (0, i))],
        out_shape=[jax.ShapeDtypeStruct((n, d), F32), jax.ShapeDtypeStruct((N_EXPERTS, n), F32)],
        compiler_params=_cparams("parallel"),
        name="merge_ln_route",
    )(x, o_pool, o_gmlp, o_nsa, w_merge, w_bp, w_bg, w_bn, w_out, ln_g, ln_b, w_router_t, b_router)


def _moe_kernel(x_ref, c_ref, wg_ref, wu_ref, wd_ref, lg_ref, lb_ref, y_ref, xb_s, acc_s, *, alpha):
    e = pl.program_id(1)

    @pl.when(e == 0)
    def _():
        xb_s[...] = x_ref[...].astype(BF16)
        acc_s[...] = jnp.zeros(acc_s.shape, acc_s.dtype)

    xb = xb_s[...]
    hg = _dot(xb, wg_ref[...])
    hu = _dot(xb, wu_ref[...])
    comb = c_ref[...]
    lane = lax.broadcasted_iota(jnp.int32, comb.shape, 1)
    ce = jnp.sum(jnp.where(lane == e, comb, 0.0), -1, keepdims=True)
    act = (hg * jax.nn.sigmoid(hg)) * hu * ce
    acc_s[...] += _dot(act.astype(BF16), wd_ref[...])

    @pl.when(e == pl.num_programs(1) - 1)
    def _():
        y_ref[...] = _layer_norm(alpha * x_ref[...] + acc_s[...], lg_ref[...], lb_ref[...])


def _moe(x, comb, w_gate, w_up, w_down, layer, ln_g, ln_b, alpha):
    n, d = x.shape
    tm = _tile(n, 512)
    ne, _, de = w_gate.shape[1:]
    return pl.pallas_call(
        functools.partial(_moe_kernel, alpha=alpha),
        grid=(n // tm, ne),
        in_specs=[pl.BlockSpec((tm, d), lambda i, e: (i, 0)),
                  pl.BlockSpec((tm, ne), lambda i, e: (i, 0)),
                  pl.BlockSpec((None, None, d, de), lambda i, e: (layer, e, 0, 0)),
                  pl.BlockSpec((None, None, d, de), lambda i, e: (layer, e, 0, 0)),
                  pl.BlockSpec((None, None, de, d), lambda i, e: (layer, e, 0, 0)),
                  pl.BlockSpec(ln_g.shape, lambda i, e: (0, 0)),
                  pl.BlockSpec(ln_b.shape, lambda i, e: (0, 0))],
        out_specs=pl.BlockSpec((tm, d), lambda i, e: (i, 0)),
        out_shape=jax.ShapeDtypeStruct((n, d), F32),
        scratch_shapes=[pltpu.VMEM((tm, d), BF16), pltpu.VMEM((tm, d), F32)],
        compiler_params=_cparams("parallel", "arbitrary"),
        name="moe_ln",
    )(x, comb, w_gate, w_up, w_down, ln_g, ln_b)


def _rope_tables(pos):
    half = ROT_DIM // 2
    inv = 1.0 / (ROPE_THETA ** (jnp.arange(half, dtype=F32) / half))
    ang = pos.astype(F32)[:, None] * inv[None, :]
    cos, sin = jnp.cos(ang), jnp.sin(ang)
    n = pos.shape[0]
    one = jnp.ones((n, HEAD_DIM - ROT_DIM), F32)
    zero = jnp.zeros((n, HEAD_DIM - ROT_DIM), F32)
    zh = jnp.zeros((n, half), F32)
    rep = LANES // HEAD_DIM
    cos_t = jnp.tile(jnp.concatenate([cos, cos, one], 1), (1, rep))
    sm_t = jnp.tile(jnp.concatenate([-sin, zh, zero], 1), (1, rep))
    sp_t = jnp.tile(jnp.concatenate([zh, sin, zero], 1), (1, rep))
    return cos_t, sm_t, sp_t


def _overlap(nc, nsb, rows):
    i = np.arange(nc)[:, None]
    j = np.arange(nsb)[None, :]
    lo = np.maximum(i * CMP_STRIDE, j * SEL_LEN)
    hi = np.minimum(i * CMP_STRIDE + CMP_LEN, (j + 1) * SEL_LEN)
    ovl = np.zeros((rows, LANES), np.float32)
    ovl[:nc, :nsb] = np.maximum(hi - lo, 0) // CMP_STRIDE
    return jnp.asarray(ovl, BF16)


def _block_diag(blocks):
    n = len(blocks)
    rows = []
    for i, b in enumerate(blocks):
        rows.append(jnp.concatenate([b if j == i else jnp.zeros((b.shape[0], blocks[j].shape[1]), b.dtype)
                                     for j in range(n)], axis=1))
    return jnp.concatenate(rows, axis=0)


def _cmp_weights(w1, pe):
    w_out, pe_out = [], []
    for kv in range(2):
        w_ab, pe_ab = [], []
        for ab in range(2):
            blocks, pes = [], []
            for r in range(CMP_STRIDE):
                rr = ab * CMP_STRIDE + r
                w_r = w1[kv, rr * HEAD_DIM:(rr + 1) * HEAD_DIM, :]
                blocks.append(_block_diag([w_r] * N_KV_HEADS))
                pes.append(jnp.tile(pe[kv, rr], N_KV_HEADS))
            w_ab.append(jnp.concatenate(blocks, axis=0))
            pe_ab.append(jnp.concatenate(pes)[None, :])
        w_out.append(jnp.stack(w_ab))
        pe_out.append(jnp.stack(pe_ab))
    return jnp.stack(w_out).astype(BF16), jnp.stack(pe_out).astype(F32)


def _chunked_kt(k, nck):
    b, t, h, d = k.shape
    return k.transpose(0, 2, 3, 1).reshape(b, h, d, nck, t // nck).transpose(0, 1, 3, 2, 4).astype(BF16)


def _chunked_v(v, nck):
    b, t, h, d = v.shape
    return v.transpose(0, 2, 1, 3).reshape(b, h, nck, t // nck, d).astype(BF16)


def kernel(x_prompt, x_sample, cache_cmp_kv, cache_sel_kv, state_win_kv, state_pool, page_table, w_in, pool_w, pool_scale, gmlp_ln_g, gmlp_ln_b, gmlp_ws, gmlp_bs, cmp_pe, cmp_w1, cmp_b1, cmp_w2, w_branch_pool, w_branch_gmlp, w_branch_nsa, w_out, ln1_g, ln1_b, ln2_g, ln2_b, w_router, b_router, w_gate, w_up, w_down):
    batch, seq, d = x_prompt.shape
    nb, tq, _ = x_sample.shape
    depth = w_in.shape[0]
    n_pool, page = cache_cmp_kv.shape[1:3]
    npages = page_table.shape[1]
    past = npages * page
    keep = state_win_kv.shape[2]
    pw = pool_scale.shape[-1]
    gw = gmlp_ln_g.shape[-1]
    kvw = N_KV_HEADS * HEAD_DIM
    n_q = N_HEADS * HEAD_DIM
    n_kv = 6 * kvw
    n_auv = pw + 2 * gw
    n_gate = 3 * N_HEADS
    alpha = (2 * depth) ** 0.25
    np_rows, ns_rows = batch * seq, nb * tq
    assert pw == gw and seq % KEY_CHUNK == 0 and seq % Q_BLOCK == 0 and tq <= CHUNK and tq % 8 == 0
    assert keep == WINDOW and keep % page == 0 and page % CMP_STRIDE == 0 and np_rows % tq == 0
    nc_p = (seq - CMP_LEN) // CMP_STRIDE + 1
    nc_s = (past + tq - CMP_LEN) // CMP_STRIDE + 1
    assert nc_p < seq // CMP_STRIDE and nc_s < past // CMP_STRIDE
    nsb_p = -(-seq // SEL_LEN)
    nsb_s = -(-(past + tq) // SEL_LEN)
    assert nsb_p <= LANES and nsb_s <= LANES

    w_main = w_in[:, :, :n_auv + n_q + n_kv].astype(BF16)
    w_g = jnp.pad(w_in[:, :, n_auv + n_q + n_kv:n_auv + n_q + n_kv + n_gate], ((0, 0), (0, 0), (0, LANES - n_gate))).astype(BF16)
    w_merge = w_in[:, :, n_auv + n_q + n_kv + n_gate:].astype(BF16)
    w_bp, w_bg, w_bn, w_o = (w.astype(BF16) for w in (w_branch_pool, w_branch_gmlp, w_branch_nsa, w_out))
    wg_b, wu_b, wd_b = w_gate.astype(BF16), w_up.astype(BF16), w_down.astype(BF16)
    w_router_t = w_router.T.astype(F32)
    b_router_c = b_router.astype(F32)[:, None]
    grp_w = gw // GMLP_GROUPS

    pos_all = jnp.concatenate([jnp.tile(jnp.arange(seq), batch), jnp.tile(past + jnp.arange(tq), nb)])
    cos_t, sm_t, sp_t = _rope_tables(pos_all)
    ovl_p = _overlap(nc_p, nsb_p, seq // CMP_STRIDE)
    ovl_s = _overlap(nc_s, nsb_s, past // CMP_STRIDE)
    n_keys_s = (npages + 1) * page
    expand_s = jnp.asarray(np.arange(LANES)[:, None] == (np.arange(n_keys_s)[None, :] // SEL_LEN), BF16)

    x_all = jnp.concatenate([x_prompt.reshape(np_rows, d), x_sample.reshape(ns_rows, d)], axis=0)
    cache_cmp2d = cache_cmp_kv.reshape(depth * n_pool * page, 2 * kvw)
    cache_sel4d = cache_sel_kv.reshape(depth, n_pool, page, 2 * kvw)
    state_win4d = state_win_kv.reshape(depth, nb, keep, 2 * kvw)
    nck = seq // KEY_CHUNK

    st_p = [[] for _ in range(4)]
    st_s = [[] for _ in range(5)]
    for l in range(depth):
        z_auv, z_q, z_kv, z_g = _proj(x_all, w_main[l], w_g[l], cos_t, sm_t, sp_t, n_auv, n_q, n_kv)

        pool_bd = _block_diag([pool_w[l, g] for g in range(len(POOL_WINDOWS))]).astype(BF16)
        ps, lg, lb = pool_scale[l][None, :], gmlp_ln_g[l][None, :], gmlp_ln_b[l][None, :]
        cl = min(CHUNK, seq)
        bs_rows = jnp.repeat(gmlp_bs[l][:, :cl].T, grp_w, axis=1)
        op_p, og_p = _mix_prompt(z_auv, batch, seq, pool_bd, ps, lg, lb, gmlp_ws[l], bs_rows)
        a_t = z_auv[np_rows:].reshape(nb, tq, n_auv).transpose(1, 0, 2)
        st_t = state_pool[l].transpose(1, 0, 2)
        wrow = jnp.repeat(gmlp_ws[l][:, :tq, :tq].transpose(1, 2, 0), grp_w, axis=2).reshape(tq * tq, gw)
        bsrow = jnp.repeat(gmlp_bs[l][:, :tq].T, grp_w, axis=1)
        op_s, og_s, vn_s = _mix_sample(a_t, st_t, float(past), pool_bd, ps, lg, lb, wrow, bsrow)
        to_rows = lambda y: y.transpose(1, 0, 2).reshape(ns_rows, -1)
        o_pool = jnp.concatenate([op_p, to_rows(op_s)], axis=0)
        o_gmlp = jnp.concatenate([og_p, to_rows(og_s)], axis=0)

        w1_bd, pe_rows = _cmp_weights(cmp_w1[l], cmp_pe[l])
        b1 = cmp_b1[l][:, None, :]
        w2kt = cmp_w2[l, 0].T.astype(BF16)
        w2v = cmp_w2[l, 1].astype(BF16)
        h1_p = _cmp1(z_kv, lambda tr: 0, np_rows, w1_bd, pe_rows)
        kct_p, vc_p = _cmp2_prompt(h1_p.reshape(batch, seq // CMP_STRIDE, -1), b1, w2kt, w2v)
        h1_pool = _cmp1(cache_cmp2d, lambda tr: l * (n_pool * page // tr), n_pool * page, w1_bd, pe_rows)
        kct_s, vc_s = _cmp2_sample(page_table, h1_pool.reshape(n_pool, page // CMP_STRIDE, -1), b1, w2kt, w2v)

        zkv_p = z_kv[:np_rows].reshape(batch, seq, 3, 2, N_KV_HEADS, HEAD_DIM)
        o_p = _nsa_prompt(z_q, z_g, kct_p, vc_p,
                          _chunked_kt(zkv_p[:, :, 1, 0], nck), _chunked_v(zkv_p[:, :, 1, 1], nck),
                          _chunked_kt(zkv_p[:, :, 2, 0], nck), _chunked_v(zkv_p[:, :, 2, 1], nck),
                          ovl_p, batch, seq)
        o_s = _nsa_sample(page_table, z_q, z_g, z_kv, np_rows, kct_s, vc_s, ovl_s, expand_s, cache_sel4d, state_win4d,
                          l, tq, past, nsb_s)
        o_nsa = jnp.concatenate([o_p, o_s.astype(BF16)], axis=0)

        x1, comb_t = _merge(x_all, o_pool, o_gmlp, o_nsa, w_merge[l], w_bp[l], w_bg[l], w_bn[l], w_o[l],
                            ln1_g[l][None, :], ln1_b[l][None, :], w_router_t, b_router_c, alpha)
        x_all = _moe(x1, comb_t.T, wg_b, wu_b, wd_b, l, ln2_g[l][None, :], ln2_b[l][None, :], alpha)

        kv_shape = (2, N_KV_HEADS, HEAD_DIM)
        zkv_s = z_kv[np_rows:].reshape(nb, tq, 3, *kv_shape)
        a_p = z_auv[:np_rows, :pw].reshape(batch, seq, pw)
        a_s = z_auv[np_rows:, :pw].reshape(nb, tq, pw)
        zkv_p = zkv_p.reshape(batch, seq, 3, *kv_shape)
        st_p[0].append(zkv_p[:, :, 0])
        st_p[1].append(zkv_p[:, :, 1])
        st_p[2].append(zkv_p[:, seq - min(WINDOW, seq):, 2])
        st_p[3].append(a_p[:, seq - POOL_KEEP:])
        st_s[0].append(zkv_s[:, :, 0])
        st_s[1].append(zkv_s[:, :, 1])
        st_s[2].append(jnp.concatenate([state_win_kv[l], zkv_s[:, :, 2]], axis=1)[:, tq:])
        st_s[3].append(jnp.concatenate([state_pool[l], a_s], axis=1)[:, tq:])
        st_s[4].append(vn_s.transpose(1, 0, 2))

    y_prompt = x_all[:np_rows].reshape(batch, seq, d)
    y_sample = x_all[np_rows:].reshape(nb, tq, d)
    return (y_prompt, y_sample,
            jnp.stack(st_p[0]), jnp.stack(st_p[1]), jnp.stack(st_p[2]), jnp.stack(st_p[3]),
            jnp.stack(st_s[0]), jnp.stack(st_s[1]), jnp.stack(st_s[2]), jnp.stack(st_s[3]), jnp.stack(st_s[4]))
```

```python
import functools
import math

import numpy as np
import jax
import jax.numpy as jnp
from jax import lax
from jax.experimental import pallas as pl
from jax.experimental.pallas import tpu as pltpu

F32 = jnp.float32
BF16 = jnp.bfloat16

POOL_WINDOWS = (2, 4, 8, 16)
POOL_KEEP = max(POOL_WINDOWS) - 1
GMLP_GROUPS = 4
CHUNK = 128
N_HEADS = 8
HEAD_DIM = 64
N_KV_HEADS = 2
GQA = N_HEADS // N_KV_HEADS
ROT_DIM = HEAD_DIM // 4
ROPE_THETA = 500000.0
CMP_LEN = 32
CMP_STRIDE = 16
CMP_HIDDEN = 2 * HEAD_DIM
SEL_LEN = 64
SEL_TOPN = 16
WINDOW = 512
Q_BLOCK = 128
N_EXPERTS = 16
EXPERTS_PER_GROUP = 4
LN_EPS = 1e-5
NEG_INF = -1e30
FORCE = 1e9

LANES = 128
KEY_CHUNK = 256
VMEM_LIMIT = 56 * 1024 * 1024


def _cparams(*sem):
    return pltpu.CompilerParams(dimension_semantics=sem, vmem_limit_bytes=VMEM_LIMIT)


def _tile(n, cap):
    t = cap
    while n % t:
        t //= 2
    return t


def _dot(a, b):
    return jnp.dot(a, b, preferred_element_type=F32)


def _dot_nt(a, b):
    return lax.dot_general(a, b, (((1,), (1,)), ((), ())), preferred_element_type=F32)


def _layer_norm(x, g, b):
    mu = jnp.mean(x, -1, keepdims=True)
    xc = x - mu
    var = jnp.mean(xc * xc, -1, keepdims=True)
    return xc * lax.rsqrt(var + LN_EPS) * g + b


def _masked_softmax(s, mask):
    sm = jnp.where(mask, s, NEG_INF)
    p = jnp.where(mask, jnp.exp(sm - jnp.max(sm, -1, keepdims=True)), 0.0)
    d = jnp.sum(p, -1, keepdims=True)
    return p / jnp.where(d > 0, d, 1.0)


def _chunked_attention(q, kt_ref, v_ref, mask_fn, c_lo, c_hi):
    rows = q.shape[0]

    def step(c, carry):
        m, l, acc = carry
        mask = mask_fn(c)
        sm = jnp.where(mask, _dot(q, kt_ref[c]), NEG_INF)
        m_new = jnp.maximum(m, jnp.max(sm, -1, keepdims=True))
        alpha = jnp.exp(m - m_new)
        p = jnp.where(mask, jnp.exp(sm - m_new), 0.0)
        l = alpha * l + jnp.sum(p, -1, keepdims=True)
        acc = alpha * acc + _dot(p.astype(BF16), v_ref[c])
        return m_new, l, acc

    init = (jnp.full((rows, 1), NEG_INF, F32), jnp.zeros((rows, 1), F32), jnp.zeros((rows, v_ref.shape[-1]), F32))
    _, l, acc = lax.fori_loop(c_lo, c_hi, step, init)
    return acc / jnp.where(l > 0, l, 1.0)


def _topk_mask(score, k):
    lane = lax.broadcasted_iota(jnp.int32, score.shape, 1).astype(F32)

    def body(_, carry):
        sc, sel = carry
        m = jnp.max(sc, -1, keepdims=True)
        first = jnp.min(jnp.where(sc == m, lane, float(2 * LANES)), -1, keepdims=True)
        pick = lane == first
        return jnp.where(pick, -jnp.inf, sc), jnp.where(pick, 1.0, sel)

    _, sel = lax.fori_loop(0, k, body, (score, jnp.zeros_like(score)), unroll=True)
    return sel


def _importance(p_cmp, rows, ovl):
    psum = p_cmp[0:rows]
    for g in range(1, GQA):
        psum = psum + p_cmp[g * rows:(g + 1) * rows]
    hi = psum.astype(BF16)
    lo = (psum - hi.astype(F32)).astype(BF16)
    return _dot(hi, ovl) + _dot(lo, ovl)


def _select_blocks(imp, qpos, nsb):
    blk = lax.broadcasted_iota(jnp.int32, imp.shape, 1)
    cur = jnp.right_shift(qpos, int(math.log2(SEL_LEN)))
    forced = (blk == 0) | (blk == cur) | (blk == cur - 1)
    score = jnp.where(forced, FORCE, jnp.where(blk <= cur, imp, -FORCE))
    score = jnp.where(blk < nsb, score, -jnp.inf)
    return _topk_mask(score, min(SEL_TOPN, nsb))


def _head_gate(gs, col):
    lane = lax.broadcasted_iota(jnp.int32, gs.shape, 1)
    return jnp.sum(jnp.where(lane == col, gs, 0.0), -1, keepdims=True)


def _proj_kernel(x_ref, w_ref, wg_ref, cos_ref, sm_ref, sp_ref, auv_ref, q_ref, kv_ref, g_ref, *, n_auv, n_q, n_kv):
    xb = x_ref[...].astype(BF16)
    z = _dot(xb, w_ref[...])
    auv_ref[...] = z[:, :n_auv]
    cos, sm, sp = cos_ref[...], sm_ref[...], sp_ref[...]
    half = ROT_DIM // 2

    def rope(t):
        return t * cos + pltpu.roll(t, LANES - half, 1) * sm + pltpu.roll(t, half, 1) * sp

    for j in range(n_q // LANES):
        c0 = n_auv + j * LANES
        q_ref[:, j * LANES:(j + 1) * LANES] = rope(z[:, c0:c0 + LANES]) * (HEAD_DIM ** -0.5)
    for j in range(n_kv // LANES):
        c0 = n_auv + n_q + j * LANES
        t = z[:, c0:c0 + LANES]
        kv_ref[:, j * LANES:(j + 1) * LANES] = rope(t) if j % 2 == 0 else t
    g_ref[...] = _dot(xb, wg_ref[...])


def _proj(x, w_main, w_gate, cos, sm, sp, n_auv, n_q, n_kv):
    n, d = x.shape
    tm = _tile(n, 512)
    row = lambda i: (i, 0)
    fixed = lambda i: (0, 0)
    return pl.pallas_call(
        functools.partial(_proj_kernel, n_auv=n_auv, n_q=n_q, n_kv=n_kv),
        grid=(n // tm,),
        in_specs=[pl.BlockSpec((tm, d), row),
                  pl.BlockSpec(w_main.shape, fixed),
                  pl.BlockSpec(w_gate.shape, fixed),
                  pl.BlockSpec((tm, LANES), row), pl.BlockSpec((tm, LANES), row), pl.BlockSpec((tm, LANES), row)],
        out_specs=[pl.BlockSpec((tm, n_auv), row), pl.BlockSpec((tm, n_q), row),
                   pl.BlockSpec((tm, n_kv), row), pl.BlockSpec((tm, LANES), row)],
        out_shape=[jax.ShapeDtypeStruct((n, n_auv), F32), jax.ShapeDtypeStruct((n, n_q), F32),
                   jax.ShapeDtypeStruct((n, n_kv), F32), jax.ShapeDtypeStruct((n, LANES), F32)],
        compiler_params=_cparams("parallel"),
        name="in_proj",
    )(x, w_main, w_gate, cos, sm, sp)


def _pool_select(grp, vals):
    out = vals[-1]
    for gi in range(len(vals) - 2, -1, -1):
        out = jnp.where(grp == gi, vals[gi], out)
    return out


def _mix_prompt_kernel(auv_ref, halo_ref, pw_ref, ps_ref, lng_ref, lnb_ref, ws_ref, bsr_ref, op_ref, og_ref, *, tm, pw):
    i = pl.program_id(1)
    halo_rows = halo_ref.shape[0]
    a = auv_ref[:, 0:pw]
    u = auv_ref[:, pw:2 * pw]
    v = auv_ref[:, 2 * pw:3 * pw]
    halo = jnp.where(i == 0, 0.0, halo_ref[...])
    ext = jnp.concatenate([halo, a], axis=0)
    sums, cur, width = [], ext, 1
    for win in POOL_WINDOWS:
        while width < win:
            cur = cur + pltpu.roll(cur, width, 0)
            width *= 2
        sums.append(cur[halo_rows:])
    lane = lax.broadcasted_iota(jnp.int32, (tm, pw), 1)
    grp = lane // (pw // len(POOL_WINDOWS))
    pos = (i * tm + lax.broadcasted_iota(jnp.int32, (tm, pw), 0)).astype(F32)
    total = _pool_select(grp, sums)
    win = _pool_select(grp, [jnp.full((tm, pw), float(w), F32) for w in POOL_WINDOWS])
    d = total / jnp.minimum(win, pos + 1.0) - a
    op_ref[...] = (_dot(d.astype(BF16), pw_ref[...]) * ps_ref[...]).astype(op_ref.dtype)

    vn = _layer_norm(v, lng_ref[...], lnb_ref[...]).astype(BF16)
    r = lax.broadcasted_iota(jnp.int32, (CHUNK, CHUNK), 0)
    c = lax.broadcasted_iota(jnp.int32, (CHUNK, CHUNK), 1)
    wms = [jnp.where(c <= r, ws_ref[g], 0.0).astype(BF16) for g in range(GMLP_GROUPS)]
    grp_c = lax.broadcasted_iota(jnp.int32, (CHUNK, pw), 1) // (pw // GMLP_GROUPS)
    for j in range(tm // CHUNK):
        rows = slice(j * CHUNK, (j + 1) * CHUNK)
        vc = vn[rows]
        mix = _pool_select(grp_c, [_dot(wms[g], vc) for g in range(GMLP_GROUPS)])
        og_ref[rows, :] = (u[rows] * (mix + bsr_ref[...])).astype(og_ref.dtype)


def _mix_prompt(z_auv, batch, seq, pool_bd, pool_scale, ln_g, ln_b, ws, bs_rows):
    pw = pool_scale.shape[-1]
    tm = _tile(seq, 512)
    nt = seq // tm
    halo = 2 * 8
    hb = tm // halo
    fixed2 = lambda b, i: (0, 0)
    return pl.pallas_call(
        functools.partial(_mix_prompt_kernel, tm=tm, pw=pw),
        grid=(batch, nt),
        in_specs=[pl.BlockSpec((tm, 3 * pw), lambda b, i: (b * nt + i, 0)),
                  pl.BlockSpec((halo, pw), lambda b, i: (jnp.maximum((b * nt + i) * hb - 1, 0), 0)),
                  pl.BlockSpec((pw, pw), fixed2), pl.BlockSpec((1, pw), fixed2),
                  pl.BlockSpec((1, pw), fixed2), pl.BlockSpec((1, pw), fixed2),
                  pl.BlockSpec(ws.shape, lambda b, i: (0, 0, 0)),
                  pl.BlockSpec((CHUNK, pw), fixed2)],
        out_specs=[pl.BlockSpec((tm, pw), lambda b, i: (b * nt + i, 0)),
                   pl.BlockSpec((tm, pw), lambda b, i: (b * nt + i, 0))],
        out_shape=[jax.ShapeDtypeStruct((batch * seq, pw), BF16), jax.ShapeDtypeStruct((batch * seq, pw), BF16)],
        compiler_params=_cparams("parallel", "parallel"),
        name="mix_prompt",
    )(z_auv, z_auv, pool_bd, pool_scale, ln_g, ln_b, ws, bs_rows)


def _mix_sample_kernel(a_ref, st_ref, pw_ref, ps_ref, lng_ref, lnb_ref, wrow_ref, bsr_ref, op_ref, og_ref, vn_ref,
                       *, tq, keep, pos0, pw):
    nb = a_ref.shape[1]
    ext = [st_ref[i] for i in range(keep)] + [a_ref[t][:, 0:pw] for t in range(tq)]
    grp = lax.broadcasted_iota(jnp.int32, (nb, pw), 1) // (pw // len(POOL_WINDOWS))
    for t in range(tq):
        e = keep + t
        acc, k, sums = ext[e], 1, []
        for win in POOL_WINDOWS:
            while k < win:
                if e - k >= 0:
                    acc = acc + ext[e - k]
                k += 1
            sums.append(acc)
        total = _pool_select(grp, sums)
        cnt = _pool_select(grp, [jnp.full((nb, pw), min(float(w), pos0 + t + 1.0), F32) for w in POOL_WINDOWS])
        d = total / cnt - ext[e]
        op_ref[t] = (_dot(d.astype(BF16), pw_ref[...]) * ps_ref[...]).astype(op_ref.dtype)
    vns = []
    for t in range(tq):
        vn = _layer_norm(a_ref[t][:, 2 * pw:3 * pw], lng_ref[...], lnb_ref[...])
        vn_ref[t] = vn
        vns.append(vn)
    for t in range(tq):
        mix = bsr_ref[t:t + 1, :]
        for s in range(t + 1):
            mix = mix + wrow_ref[t * tq + s:t * tq + s + 1, :] * vns[s]
        og_ref[t] = (a_ref[t][:, pw:2 * pw] * mix).astype(og_ref.dtype)


def _mix_sample(a_t, st_t, pos0, pool_bd, pool_scale, ln_g, ln_b, wrow, bsrow):
    tq, nb, w3 = a_t.shape
    keep = st_t.shape[0]
    pw = w3 // 3
    full = lambda shape: pl.BlockSpec(shape, lambda i: (0,) * len(shape))
    return pl.pallas_call(
        functools.partial(_mix_sample_kernel, tq=tq, keep=keep, pos0=pos0, pw=pw),
        grid=(1,),
        in_specs=[full(a_t.shape), full(st_t.shape), full(pool_bd.shape), full(pool_scale.shape),
                  full(ln_g.shape), full(ln_b.shape), full(wrow.shape), full(bsrow.shape)],
        out_specs=[full((tq, nb, pw)), full((tq, nb, pw)), full((tq, nb, pw))],
        out_shape=[jax.ShapeDtypeStruct((tq, nb, pw), BF16), jax.ShapeDtypeStruct((tq, nb, pw), BF16),
                   jax.ShapeDtypeStruct((tq, nb, pw), F32)],
        compiler_params=_cparams("arbitrary"),
        name="mix_sample",
    )(a_t, st_t, pool_bd, pool_scale, ln_g, ln_b, wrow, bsrow)


def _cmp1_kernel(x_ref, w_ref, pe_ref, h_ref, *, g):
    hw = N_KV_HEADS * CMP_HIDDEN
    xcat = jnp.concatenate([x_ref[pl.ds(r, g, stride=CMP_STRIDE), :] for r in range(CMP_STRIDE)], axis=1)
    for ab in range(2):
        lhs = (xcat + pe_ref[ab]).astype(BF16)
        h_ref[:, ab * hw:(ab + 1) * hw] = _dot(lhs, w_ref[ab])


def _cmp1(x2d, row_block0, n_rows, w1_bd, pe_rows):
    tr = _tile(n_rows, 4096)
    g = tr // CMP_STRIDE
    kvw = N_KV_HEADS * HEAD_DIM
    hcols = 2 * N_KV_HEADS * CMP_HIDDEN
    return pl.pallas_call(
        functools.partial(_cmp1_kernel, g=g),
        grid=(n_rows // tr, 2),
        in_specs=[pl.BlockSpec((tr, kvw), lambda i, kv: (row_block0(tr) + i, kv)),
                  pl.BlockSpec((None,) + w1_bd.shape[1:], lambda i, kv: (kv, 0, 0, 0)),
                  pl.BlockSpec((None,) + pe_rows.shape[1:], lambda i, kv: (kv, 0, 0, 0))],
        out_specs=pl.BlockSpec((g, hcols), lambda i, kv: (i, kv)),
        out_shape=jax.ShapeDtypeStruct((n_rows // CMP_STRIDE, 2 * hcols), F32),
        compiler_params=_cparams("parallel", "parallel"),
        name="cmp_stage1",
    )(x2d, w1_bd, pe_rows)


def _cmp1_pages_kernel(xt_ref, w_ref, pe_ref, h_ref, x_sc, *, g):
    n_pages, n_h, hd, page = xt_ref.shape
    for p in range(n_pages):
        x_sc[p * page:(p + 1) * page, :] = xt_ref[p].reshape(n_h * hd, page).T
    _cmp1_kernel(x_sc, w_ref, pe_ref, h_ref, g=g)


def _cmp1_pages(cache_t, layer, w1_bd, pe_rows):
    _, n_pool, _, n_h, hd, page = cache_t.shape
    pp = _tile(n_pool, 4096 // page)
    g = pp * page // CMP_STRIDE
    hcols = 2 * N_KV_HEADS * CMP_HIDDEN
    return pl.pallas_call(
        functools.partial(_cmp1_pages_kernel, g=g),
        grid=(n_pool // pp, 2),
        in_specs=[pl.BlockSpec((None, pp, None, n_h, hd, page), lambda i, kv: (layer, i, kv, 0, 0, 0)),
                  pl.BlockSpec((None,) + w1_bd.shape[1:], lambda i, kv: (kv, 0, 0, 0)),
                  pl.BlockSpec((None,) + pe_rows.shape[1:], lambda i, kv: (kv, 0, 0, 0))],
        out_specs=pl.BlockSpec((g, hcols), lambda i, kv: (i, kv)),
        out_shape=jax.ShapeDtypeStruct((n_pool * page // CMP_STRIDE, 2 * hcols), F32),
        scratch_shapes=[pltpu.VMEM((pp * page, n_h * hd), F32)],
        compiler_params=_cparams("parallel", "parallel"),
        name="cmp_stage1_pages",
    )(cache_t, w1_bd, pe_rows)


def _gelu_tanh(x):
    return 0.5 * x * (1.0 + jnp.tanh(math.sqrt(2.0 / math.pi) * (x + 0.044715 * (x * x * x))))


def _cmp2_body(h_refs, b1_ref, w2kt_ref, w2v_ref, kct_ref, vc_ref):
    hall = jnp.concatenate([r[...] for r in h_refs], axis=0) if len(h_refs) > 1 else h_refs[0][...]
    g = hall.shape[0]
    hw = N_KV_HEADS * CMP_HIDDEN
    for kv in range(2):
        for h in range(N_KV_HEADS):
            ca = (kv * 2) * hw + h * CMP_HIDDEN
            cb = (kv * 2 + 1) * hw + h * CMP_HIDDEN
            hid = hall[:, ca:ca + CMP_HIDDEN] + pltpu.roll(hall[:, cb:cb + CMP_HIDDEN], g - 1, 0) + b1_ref[kv]
            hid = _gelu_tanh(hid).astype(BF16)
            if kv == 0:
                kct_ref[h] = _dot_nt(w2kt_ref[...], hid).astype(kct_ref.dtype)
            else:
                vc_ref[h] = _dot(hid, w2v_ref[...]).astype(vc_ref.dtype)


def _cmp2_prompt_kernel(h_ref, b1_ref, w2kt_ref, w2v_ref, kct_ref, vc_ref):
    _cmp2_body([h_ref], b1_ref, w2kt_ref, w2v_ref, kct_ref, vc_ref)


def _cmp2_sample_kernel(pt_ref, *refs, npages):
    _cmp2_body(list(refs[:npages]), *refs[npages:])


def _cmp2_out(n, g):
    specs = [pl.BlockSpec((None, N_KV_HEADS, HEAD_DIM, g), lambda b, *_: (b, 0, 0, 0)),
             pl.BlockSpec((None, N_KV_HEADS, g, HEAD_DIM), lambda b, *_: (b, 0, 0, 0))]
    shapes = [jax.ShapeDtypeStruct((n, N_KV_HEADS, HEAD_DIM, g), BF16),
              jax.ShapeDtypeStruct((n, N_KV_HEADS, g, HEAD_DIM), BF16)]
    return specs, shapes


def _cmp2_prompt(h1, b1, w2kt, w2v):
    n, g, hc = h1.shape
    out_specs, out_shape = _cmp2_out(n, g)
    return pl.pallas_call(
        _cmp2_prompt_kernel,
        grid=(n,),
        in_specs=[pl.BlockSpec((None, g, hc), lambda b: (b, 0, 0)),
                  pl.BlockSpec(b1.shape, lambda b: (0, 0, 0)),
                  pl.BlockSpec(w2kt.shape, lambda b: (0, 0)),
                  pl.BlockSpec(w2v.shape, lambda b: (0, 0))],
        out_specs=out_specs, out_shape=out_shape,
        compiler_params=_cparams("parallel"),
        name="cmp_stage2_prompt",
    )(h1, b1, w2kt, w2v)


def _cmp2_sample(page_table, h1_pool, b1, w2kt, w2v):
    n, npages = page_table.shape
    _, gp, hc = h1_pool.shape
    out_specs, out_shape = _cmp2_out(n, npages * gp)
    page_specs = [pl.BlockSpec((None, gp, hc), functools.partial(lambda b, pt, j: (pt[b, j], 0, 0), j=j))
                  for j in range(npages)]
    return pl.pallas_call(
        functools.partial(_cmp2_sample_kernel, npages=npages),
        grid_spec=pltpu.PrefetchScalarGridSpec(
            num_scalar_prefetch=1, grid=(n,),
            in_specs=page_specs + [pl.BlockSpec(b1.shape, lambda b, pt: (0, 0, 0)),
                                   pl.BlockSpec(w2kt.shape, lambda b, pt: (0, 0)),
                                   pl.BlockSpec(w2v.shape, lambda b, pt: (0, 0))],
            out_specs=out_specs),
        out_shape=out_shape,
        compiler_params=_cparams("parallel"),
        name="cmp_stage2_sample",
    )(page_table, *([h1_pool] * npages), b1, w2kt, w2v)


def _nsa_prompt_kernel(q_ref, g_ref, kct_ref, vc_ref, kts_ref, vs_ref, ktw_ref, vw_ref, ovl_ref, o_ref, *, nsb):
    h = pl.program_id(1)
    qb = pl.program_id(2)
    tq = Q_BLOCK
    rows = GQA * tq
    t0 = qb * tq
    qf = q_ref[...]
    q = jnp.concatenate([qf[:, g * HEAD_DIM:(g + 1) * HEAD_DIM] for g in range(GQA)], axis=0).astype(BF16)
    qpos_r = t0 + (lax.broadcasted_iota(jnp.int32, (rows, 1), 0) & (tq - 1))

    ncp = kct_ref.shape[-1]
    s = _dot(q, kct_ref[...])
    cend = lax.broadcasted_iota(jnp.int32, (1, ncp), 1) * CMP_STRIDE + (CMP_LEN - 1)
    p_cmp = _masked_softmax(s, cend <= qpos_r)
    o_cmp = _dot(p_cmp.astype(BF16), vc_ref[...])

    imp = _importance(p_cmp, tq, ovl_ref[...])
    qpos = t0 + lax.broadcasted_iota(jnp.int32, imp.shape, 0)
    selm = _select_blocks(imp, qpos, nsb).astype(BF16)
    per_chunk = KEY_CHUNK // SEL_LEN
    jrow = lax.broadcasted_iota(jnp.int32, (selm.shape[1], KEY_CHUNK), 0)
    kblk = jnp.right_shift(lax.broadcasted_iota(jnp.int32, (selm.shape[1], KEY_CHUNK), 1), int(math.log2(SEL_LEN)))
    kcol = lax.broadcasted_iota(jnp.int32, (1, KEY_CHUNK), 1)

    def sel_mask(c):
        expand = (jrow == c * per_chunk + kblk).astype(BF16)
        e = _dot(selm, expand)
        e = jnp.concatenate([e] * GQA, axis=0)
        return (e > 0.5) & (c * KEY_CHUNK + kcol <= qpos_r)

    def win_mask(c):
        kpos = c * KEY_CHUNK + kcol
        return (kpos <= qpos_r) & (kpos > qpos_r - WINDOW)

    c_hi = (t0 + tq - 1) // KEY_CHUNK + 1
    c_lo = jnp.maximum(t0 - WINDOW + 1, 0) // KEY_CHUNK
    o_sel = _chunked_attention(q, kts_ref, vs_ref, sel_mask, 0, c_hi)
    o_win = _chunked_attention(q, ktw_ref, vw_ref, win_mask, c_lo, c_hi)

    gs = jax.nn.sigmoid(g_ref[...])
    outs = []
    for g in range(GQA):
        r = slice(g * tq, (g + 1) * tq)
        col = (h * GQA + g) * 3
        outs.append(_head_gate(gs, col) * o_cmp[r] + _head_gate(gs, col + 1) * o_sel[r] + _head_gate(gs, col + 2) * o_win[r])
    o_ref[...] = jnp.concatenate(outs, axis=1).astype(o_ref.dtype)


def _nsa_prompt(z_q, z_g, kct, vc, kts, vs, ktw, vw, ovl, batch, seq):
    nqb = seq // Q_BLOCK
    hq = GQA * HEAD_DIM
    nsb = -(-seq // SEL_LEN)
    per_head = lambda shape: pl.BlockSpec((None, None) + shape, lambda b, h, i: (b, h) + (0,) * len(shape))
    return pl.pallas_call(
        functools.partial(_nsa_prompt_kernel, nsb=nsb),
        grid=(batch, N_KV_HEADS, nqb),
        in_specs=[pl.BlockSpec((Q_BLOCK, hq), lambda b, h, i: (b * nqb + i, h)),
                  pl.BlockSpec((Q_BLOCK, LANES), lambda b, h, i: (b * nqb + i, 0)),
                  per_head(kct.shape[2:]), per_head(vc.shape[2:]),
                  per_head(kts.shape[2:]), per_head(vs.shape[2:]),
                  per_head(ktw.shape[2:]), per_head(vw.shape[2:]),
                  pl.BlockSpec(ovl.shape, lambda b, h, i: (0, 0))],
        out_specs=pl.BlockSpec((Q_BLOCK, hq), lambda b, h, i: (b * nqb + i, h)),
        out_shape=jax.ShapeDtypeStruct((batch * seq, N_KV_HEADS * hq), BF16),
        compiler_params=_cparams("parallel", "parallel", "parallel"),
        name="nsa_prompt",
    )(z_q, z_g, kct, vc, kts, vs, ktw, vw, ovl)


def _nsa_sample_kernel(pt_ref, q_ref, g_ref, kvn_ref, kct_ref, vc_ref, ovl_ref, exp_ref, win_ref, *rest,
                       npages, tq, pos0, nsb):
    page_refs = rest[:npages]
    o_ref = rest[npages]
    newpg = rest[npages + 1]
    page = newpg.shape[0]
    kvw = N_KV_HEADS * HEAD_DIM
    newpg[...] = jnp.zeros(newpg.shape, newpg.dtype)
    newpg[0:tq, :] = kvn_ref[...]
    rows = GQA * tq
    qpos_r = pos0 + lax.rem(lax.broadcasted_iota(jnp.int32, (rows, 1), 0), tq)
    qf = q_ref[...]
    gs = jax.nn.sigmoid(g_ref[...])
    keep = win_ref.shape[-1]
    outs = []
    for h in range(N_KV_HEADS):
        q = jnp.concatenate([qf[:, (h * GQA + g) * HEAD_DIM:(h * GQA + g + 1) * HEAD_DIM] for g in range(GQA)],
                            axis=0).astype(BF16)
        kc0 = h * HEAD_DIM
        vc0 = kvw + h * HEAD_DIM

        ncp = kct_ref.shape[-1]
        s = _dot(q, kct_ref[h])
        cend = lax.broadcasted_iota(jnp.int32, (1, ncp), 1) * CMP_STRIDE + (CMP_LEN - 1)
        p_cmp = _masked_softmax(s, cend <= qpos_r)
        o_cmp = _dot(p_cmp.astype(BF16), vc_ref[h])

        imp = _importance(p_cmp, tq, ovl_ref[...])
        qpos = pos0 + lax.broadcasted_iota(jnp.int32, imp.shape, 0)
        selm = _select_blocks(imp, qpos, nsb).astype(BF16)
        e = _dot(jnp.concatenate([selm] * GQA, axis=0), exp_ref[...])
        k_new = newpg[:, 2 * kvw + kc0:2 * kvw + kc0 + HEAD_DIM].astype(BF16)
        v_new = newpg[:, 2 * kvw + vc0:2 * kvw + vc0 + HEAD_DIM].astype(BF16)
        s = jnp.concatenate([_dot(q, page_refs[p][0, h].astype(BF16)) for p in range(npages)] + [_dot_nt(q, k_new)], axis=1)
        kpos = lax.broadcasted_iota(jnp.int32, (1, s.shape[1]), 1)
        p = _masked_softmax(s, (e > 0.5) & (kpos <= qpos_r)).astype(BF16)
        o_sel = _dot(p[:, npages * page:], v_new)
        for j in range(npages):
            o_sel = o_sel + _dot_nt(p[:, j * page:(j + 1) * page], page_refs[j][1, h].astype(BF16))

        k_new = newpg[:, 4 * kvw + kc0:4 * kvw + kc0 + HEAD_DIM].astype(BF16)
        v_new = newpg[:, 4 * kvw + vc0:4 * kvw + vc0 + HEAD_DIM].astype(BF16)
        s = jnp.concatenate([_dot(q, win_ref[0, h].astype(BF16)), _dot_nt(q, k_new)], axis=1)
        kpos = pos0 - keep + lax.broadcasted_iota(jnp.int32, (1, s.shape[1]), 1)
        p = _masked_softmax(s, (kpos <= qpos_r) & (kpos > qpos_r - WINDOW)).astype(BF16)
        o_win = _dot_nt(p[:, :keep], win_ref[1, h].astype(BF16)) + _dot(p[:, keep:], v_new)

        for g in range(GQA):
            r = slice(g * tq, (g + 1) * tq)
            col = (h * GQA + g) * 3
            outs.append(_head_gate(gs, col) * o_cmp[r] + _head_gate(gs, col + 1) * o_sel[r]
                        + _head_gate(gs, col + 2) * o_win[r])
    o_ref[...] = jnp.concatenate(outs, axis=1).astype(o_ref.dtype)


def _nsa_sample(page_table, z_q, z_g, z_kv, row0, kct, vc, ovl, expand, cache_sel, state_win, layer, tq, pos0, nsb):
    n, npages = page_table.shape
    page = cache_sel.shape[-1]
    blk0 = row0 // tq
    hq = z_q.shape[1]
    page_specs = [pl.BlockSpec((None, None) + cache_sel.shape[2:],
                               functools.partial(lambda b, pt, j: (layer, pt[b, j], 0, 0, 0, 0), j=j))
                  for j in range(npages)]
    return pl.pallas_call(
        functools.partial(_nsa_sample_kernel, npages=npages, tq=tq, pos0=pos0, nsb=nsb),
        grid_spec=pltpu.PrefetchScalarGridSpec(
            num_scalar_prefetch=1, grid=(n,),
            in_specs=[pl.BlockSpec((tq, hq), lambda b, pt: (blk0 + b, 0)),
                      pl.BlockSpec((tq, LANES), lambda b, pt: (blk0 + b, 0)),
                      pl.BlockSpec((tq, z_kv.shape[1]), lambda b, pt: (blk0 + b, 0)),
                      pl.BlockSpec((None,) + kct.shape[1:], lambda b, pt: (b, 0, 0, 0)),
                      pl.BlockSpec((None,) + vc.shape[1:], lambda b, pt: (b, 0, 0, 0)),
                      pl.BlockSpec(ovl.shape, lambda b, pt: (0, 0)),
                      pl.BlockSpec(expand.shape, lambda b, pt: (0, 0)),
                      pl.BlockSpec((None, None) + state_win.shape[2:], lambda b, pt: (layer, b, 0, 0, 0, 0))] + page_specs,
            out_specs=pl.BlockSpec((tq, hq), lambda b, pt: (b, 0)),
            scratch_shapes=[pltpu.VMEM((page, z_kv.shape[1]), F32)]),
        out_shape=jax.ShapeDtypeStruct((n * tq, hq), F32),
        compiler_params=_cparams("parallel"),
        name="nsa_sample",
    )(page_table, z_q, z_g, z_kv, kct, vc, ovl, expand, state_win, *([cache_sel] * npages))


def _route(x1, wrt, br, ct_ref):
    logits = lax.dot_general(wrt, x1, (((1,), (1,)), ((), ())), precision=lax.Precision.HIGHEST,
                             preferred_element_type=F32)
    aff = jax.nn.sigmoid(logits)
    score = aff + br
    sc = [score[e:e + 1, :] for e in range(N_EXPERTS)]
    af = [aff[e:e + 1, :] for e in range(N_EXPERTS)]
    n_grp = N_EXPERTS // EXPERTS_PER_GROUP
    gsum = []
    for k in range(n_grp):
        m = sc[k * EXPERTS_PER_GROUP:(k + 1) * EXPERTS_PER_GROUP]
        best = None
        for i in range(EXPERTS_PER_GROUP):
            for j in range(i + 1, EXPERTS_PER_GROUP):
                pair = m[i] + m[j]
                best = pair if best is None else jnp.maximum(best, pair)
        gsum.append(best)
    is_best, taken = [], None
    for k in range(n_grp):
        ok = None
        for j in range(k + 1, n_grp):
            c = gsum[k] >= gsum[j]
            ok = c if ok is None else ok & c
        if ok is None:
            ok = jnp.ones_like(gsum[k], dtype=jnp.bool_)
        if taken is not None:
            ok = ok & jnp.logical_not(taken)
        is_best.append(ok)
        taken = ok if taken is None else taken | ok
    sel = []
    for e in range(N_EXPERTS):
        k = e // EXPERTS_PER_GROUP
        rank = jnp.zeros_like(sc[e])
        for e2 in range(k * EXPERTS_PER_GROUP, (k + 1) * EXPERTS_PER_GROUP):
            if e2 == e:
                continue
            ahead = (sc[e2] >= sc[e]) if e2 < e else (sc[e2] > sc[e])
            rank = rank + ahead.astype(F32)
        sel.append(is_best[k] & (rank < 2.0))
    denom = jnp.zeros_like(sc[0])
    for e in range(N_EXPERTS):
        denom = denom + jnp.where(sel[e], af[e], 0.0)
    for e in range(N_EXPERTS):
        ct_ref[e:e + 1, :] = jnp.where(sel[e], af[e] / denom, 0.0)


def _merge_kernel(x_ref, op_ref, og_ref, on_ref, wm_ref, wp_ref, wgm_ref, wn_ref, wo_ref, lg_ref, lb_ref,
                  wrt_ref, br_ref, x1_ref, ct_ref, *, alpha):
    x = x_ref[...]
    xb = x.astype(BF16)
    d = x.shape[1]
    merged = None
    for b, (o_r, w_r) in enumerate(((op_ref, wp_ref), (og_ref, wgm_ref), (on_ref, wn_ref))):
        gate = jax.nn.sigmoid(_dot(xb, wm_ref[:, b * d:(b + 1) * d]))
        term = gate * _dot(o_r[...], w_r[...])
        merged = term if merged is None else merged + term
    out = _dot(merged.astype(BF16), wo_ref[...])
    x1 = _layer_norm(alpha * x + out, lg_ref[...], lb_ref[...])
    x1_ref[...] = x1
    _route(x1, wrt_ref[...], br_ref[...], ct_ref)


def _merge(x, o_pool, o_gmlp, o_nsa, w_merge, w_bp, w_bg, w_bn, w_out, ln_g, ln_b, w_router_t, b_router, alpha):
    n, d = x.shape
    tm = _tile(n, 512)
    row = lambda i: (i, 0)
    fixed = lambda a: pl.BlockSpec(a.shape, lambda i: (0, 0))
    return pl.pallas_call(
        functools.partial(_merge_kernel, alpha=alpha),
        grid=(n // tm,),
        in_specs=[pl.BlockSpec((tm, d), row), pl.BlockSpec((tm, o_pool.shape[1]), row),
                  pl.BlockSpec((tm, o_gmlp.shape[1]), row), pl.BlockSpec((tm, o_nsa.shape[1]), row),
                  fixed(w_merge), fixed(w_bp), fixed(w_bg), fixed(w_bn), fixed(w_out), fixed(ln_g), fixed(ln_b),
                  fixed(w_router_t), fixed(b_router)],
        out_specs=[pl.BlockSpec((tm, d), row), pl.BlockSpec((N_EXPERTS, tm), lambda i: (0, i))],
        out_shape=[jax.ShapeDtypeStruct((n, d), F32), jax.ShapeDtypeStruct((N_EXPERTS, n), F32)],
        compiler_params=_cparams("parallel"),
        name="merge_ln_route",
    )(x, o_pool, o_gmlp, o_nsa, w_merge, w_bp, w_bg, w_bn, w_out, ln_g, ln_b, w_router_t, b_router)


def _moe_kernel(x_ref, c_ref, wg_ref, wu_ref, wd_ref, lg_ref, lb_ref, y_ref, xb_s, acc_s, *, alpha):
    e = pl.program_id(1)

    @pl.when(e == 0)
    def _():
        xb_s[...] = x_ref[...].astype(BF16)
        acc_s[...] = jnp.zeros(acc_s.shape, acc_s.dtype)

    xb = xb_s[...]
    hg = _dot(xb, wg_ref[...])
    hu = _dot(xb, wu_ref[...])
    comb = c_ref[...]
    lane = lax.broadcasted_iota(jnp.int32, comb.shape, 1)
    ce = jnp.sum(jnp.where(lane == e, comb, 0.0), -1, keepdims=True)
    act = (hg * jax.nn.sigmoid(hg)) * hu * ce
    acc_s[...] += _dot(act.astype(BF16), wd_ref[...])

    @pl.when(e == pl.num_programs(1) - 1)
    def _():
        y_ref[...] = _layer_norm(alpha * x_ref[...] + acc_s[...], lg_ref[...], lb_ref[...])


def _moe(x, comb, w_gate, w_up, w_down, layer, ln_g, ln_b, alpha):
    n, d = x.shape
    tm = _tile(n, 512)
    ne, _, de = w_gate.shape[1:]
    return pl.pallas_call(
        functools.partial(_moe_kernel, alpha=alpha),
        grid=(n // tm, ne),
        in_specs=[pl.BlockSpec((tm, d), lambda i, e: (i, 0)),
                  pl.BlockSpec((tm, ne), lambda i, e: (i, 0)),
                  pl.BlockSpec((None, None, d, de), lambda i, e: (layer, e, 0, 0)),
                  pl.BlockSpec((None, None, d, de), lambda i, e: (layer, e, 0, 0)),
                  pl.BlockSpec((None, None, de, d), lambda i, e: (layer, e, 0, 0)),
                  pl.BlockSpec(ln_g.shape, lambda i, e: (0, 0)),
                  pl.BlockSpec(ln_b.shape, lambda i, e: (0, 0))],
        out_specs=pl.BlockSpec((tm, d), lambda i, e: (i, 0)),
        out_shape=jax.ShapeDtypeStruct((n, d), F32),
        scratch_shapes=[pltpu.VMEM((tm, d), BF16), pltpu.VMEM((tm, d), F32)],
        compiler_params=_cparams("parallel", "arbitrary"),
        name="moe_ln",
    )(x, comb, w_gate, w_up, w_down, ln_g, ln_b)


def _rope_tables(pos):
    half = ROT_DIM // 2
    inv = 1.0 / (ROPE_THETA ** (jnp.arange(half, dtype=F32) / half))
    ang = pos.astype(F32)[:, None] * inv[None, :]
    cos, sin = jnp.cos(ang), jnp.sin(ang)
    n = pos.shape[0]
    one = jnp.ones((n, HEAD_DIM - ROT_DIM), F32)
    zero = jnp.zeros((n, HEAD_DIM - ROT_DIM), F32)
    zh = jnp.zeros((n, half), F32)
    rep = LANES // HEAD_DIM
    cos_t = jnp.tile(jnp.concatenate([cos, cos, one], 1), (1, rep))
    sm_t = jnp.tile(jnp.concatenate([-sin, zh, zero], 1), (1, rep))
    sp_t = jnp.tile(jnp.concatenate([zh, sin, zero], 1), (1, rep))
    return cos_t, sm_t, sp_t


def _overlap(nc, nsb, rows):
    i = np.arange(nc)[:, None]
    j = np.arange(nsb)[None, :]
    lo = np.maximum(i * CMP_STRIDE, j * SEL_LEN)
    hi = np.minimum(i * CMP_STRIDE + CMP_LEN, (j + 1) * SEL_LEN)
    ovl = np.zeros((rows, LANES), np.float32)
    ovl[:nc, :nsb] = np.maximum(hi - lo, 0) // CMP_STRIDE
    return jnp.asarray(ovl, BF16)


def _block_diag(blocks):
    n = len(blocks)
    rows = []
    for i, b in enumerate(blocks):
        rows.append(jnp.concatenate([b if j == i else jnp.zeros((b.shape[0], blocks[j].shape[1]), b.dtype)
                                     for j in range(n)], axis=1))
    return jnp.concatenate(rows, axis=0)


def _cmp_weights(w1, pe):
    w_out, pe_out = [], []
    for kv in range(2):
        w_ab, pe_ab = [], []
        for ab in range(2):
            blocks, pes = [], []
            for r in range(CMP_STRIDE):
                rr = ab * CMP_STRIDE + r
                w_r = w1[kv, rr * HEAD_DIM:(rr + 1) * HEAD_DIM, :]
                blocks.append(_block_diag([w_r] * N_KV_HEADS))
                pes.append(jnp.tile(pe[kv, rr], N_KV_HEADS))
            w_ab.append(jnp.concatenate(blocks, axis=0))
            pe_ab.append(jnp.concatenate(pes)[None, :])
        w_out.append(jnp.stack(w_ab))
        pe_out.append(jnp.stack(pe_ab))
    return jnp.stack(w_out).astype(BF16), jnp.stack(pe_out).astype(F32)


def _chunked_kt(k, nck):
    b, t, h, d = k.shape
    return k.transpose(0, 2, 3, 1).reshape(b, h, d, nck, t // nck).transpose(0, 1, 3, 2, 4).astype(BF16)


def _chunked_v(v, nck):
    b, t, h, d = v.shape
    return v.transpose(0, 2, 1, 3).reshape(b, h, nck, t // nck, d).astype(BF16)


def kernel(x_prompt, x_sample, cache_cmp_kv, cache_sel_kv, state_win_kv, state_pool, page_table, w_in, pool_w, pool_scale, gmlp_ln_g, gmlp_ln_b, gmlp_ws, gmlp_bs, cmp_pe, cmp_w1, cmp_b1, cmp_w2, w_branch_pool, w_branch_gmlp, w_branch_nsa, w_out, ln1_g, ln1_b, ln2_g, ln2_b, w_router, b_router, w_gate, w_up, w_down):
    batch, seq, d = x_prompt.shape
    nb, tq, _ = x_sample.shape
    depth = w_in.shape[0]
    n_pool, page = cache_cmp_kv.shape[1:3]
    npages = page_table.shape[1]
    past = npages * page
    keep = state_win_kv.shape[2]
    pw = pool_scale.shape[-1]
    gw = gmlp_ln_g.shape[-1]
    kvw = N_KV_HEADS * HEAD_DIM
    n_q = N_HEADS * HEAD_DIM
    n_kv = 6 * kvw
    n_auv = pw + 2 * gw
    n_gate = 3 * N_HEADS
    alpha = (2 * depth) ** 0.25
    np_rows, ns_rows = batch * seq, nb * tq
    assert pw == gw and seq % KEY_CHUNK == 0 and seq % Q_BLOCK == 0 and tq <= CHUNK and tq % 8 == 0
    assert keep == WINDOW and keep % page == 0 and page % CMP_STRIDE == 0 and np_rows % tq == 0
    nc_p = (seq - CMP_LEN) // CMP_STRIDE + 1
    nc_s = (past + tq - CMP_LEN) // CMP_STRIDE + 1
    assert nc_p < seq // CMP_STRIDE and nc_s < past // CMP_STRIDE
    nsb_p = -(-seq // SEL_LEN)
    nsb_s = -(-(past + tq) // SEL_LEN)
    assert nsb_p <= LANES and nsb_s <= LANES

    w_main = w_in[:, :, :n_auv + n_q + n_kv].astype(BF16)
    w_g = jnp.pad(w_in[:, :, n_auv + n_q + n_kv:n_auv + n_q + n_kv + n_gate], ((0, 0), (0, 0), (0, LANES - n_gate))).astype(BF16)
    w_merge = w_in[:, :, n_auv + n_q + n_kv + n_gate:].astype(BF16)
    w_bp, w_bg, w_bn, w_o = (w.astype(BF16) for w in (w_branch_pool, w_branch_gmlp, w_branch_nsa, w_out))
    wg_b, wu_b, wd_b = w_gate.astype(BF16), w_up.astype(BF16), w_down.astype(BF16)
    w_router_t = w_router.T.astype(F32)
    b_router_c = b_router.astype(F32)[:, None]
    grp_w = gw // GMLP_GROUPS

    pos_all = jnp.concatenate([jnp.tile(jnp.arange(seq), batch), jnp.tile(past + jnp.arange(tq), nb)])
    cos_t, sm_t, sp_t = _rope_tables(pos_all)
    ovl_p = _overlap(nc_p, nsb_p, seq // CMP_STRIDE)
    ovl_s = _overlap(nc_s, nsb_s, past // CMP_STRIDE)
    n_keys_s = (npages + 1) * page
    expand_s = jnp.asarray(np.arange(LANES)[:, None] == (np.arange(n_keys_s)[None, :] // SEL_LEN), BF16)

    x_all = jnp.concatenate([x_prompt.reshape(np_rows, d), x_sample.reshape(ns_rows, d)], axis=0)
    rows_last = (0, 1, 3, 4, 5, 2)
    cache_cmp_t = cache_cmp_kv.transpose(rows_last)
    cache_sel_t = cache_sel_kv.transpose(rows_last)
    state_win_t = state_win_kv.transpose(rows_last)
    nck = seq // KEY_CHUNK

    st_p = [[] for _ in range(4)]
    st_s = [[] for _ in range(5)]
    for l in range(depth):
        z_auv, z_q, z_kv, z_g = _proj(x_all, w_main[l], w_g[l], cos_t, sm_t, sp_t, n_auv, n_q, n_kv)

        pool_bd = _block_diag([pool_w[l, g] for g in range(len(POOL_WINDOWS))]).astype(BF16)
        ps, lg, lb = pool_scale[l][None, :], gmlp_ln_g[l][None, :], gmlp_ln_b[l][None, :]
        cl = min(CHUNK, seq)
        bs_rows = jnp.repeat(gmlp_bs[l][:, :cl].T, grp_w, axis=1)
        op_p, og_p = _mix_prompt(z_auv, batch, seq, pool_bd, ps, lg, lb, gmlp_ws[l], bs_rows)
        a_t = z_auv[np_rows:].reshape(nb, tq, n_auv).transpose(1, 0, 2)
        st_t = state_pool[l].transpose(1, 0, 2)
        wrow = jnp.repeat(gmlp_ws[l][:, :tq, :tq].transpose(1, 2, 0), grp_w, axis=2).reshape(tq * tq, gw)
        bsrow = jnp.repeat(gmlp_bs[l][:, :tq].T, grp_w, axis=1)
        op_s, og_s, vn_s = _mix_sample(a_t, st_t, float(past), pool_bd, ps, lg, lb, wrow, bsrow)
        to_rows = lambda y: y.transpose(1, 0, 2).reshape(ns_rows, -1)
        o_pool = jnp.concatenate([op_p, to_rows(op_s)], axis=0)
        o_gmlp = jnp.concatenate([og_p, to_rows(og_s)], axis=0)

        w1_bd, pe_rows = _cmp_weights(cmp_w1[l], cmp_pe[l])
        b1 = cmp_b1[l][:, None, :]
        w2kt = cmp_w2[l, 0].T.astype(BF16)
        w2v = cmp_w2[l, 1].astype(BF16)
        h1_p = _cmp1(z_kv, lambda tr: 0, np_rows, w1_bd, pe_rows)
        kct_p, vc_p = _cmp2_prompt(h1_p.reshape(batch, seq // CMP_STRIDE, -1), b1, w2kt, w2v)
        h1_pool = _cmp1_pages(cache_cmp_t, l, w1_bd, pe_rows)
        kct_s, vc_s = _cmp2_sample(page_table, h1_pool.reshape(n_pool, page // CMP_STRIDE, -1), b1, w2kt, w2v)

        zkv_p = z_kv[:np_rows].reshape(batch, seq, 3, 2, N_KV_HEADS, HEAD_DIM)
        o_p = _nsa_prompt(z_q, z_g, kct_p, vc_p,
                          _chunked_kt(zkv_p[:, :, 1, 0], nck), _chunked_v(zkv_p[:, :, 1, 1], nck),
                          _chunked_kt(zkv_p[:, :, 2, 0], nck), _chunked_v(zkv_p[:, :, 2, 1], nck),
                          ovl_p, batch, seq)
        o_s = _nsa_sample(page_table, z_q, z_g, z_kv, np_rows, kct_s, vc_s, ovl_s, expand_s, cache_sel_t, state_win_t,
                          l, tq, past, nsb_s)
        o_nsa = jnp.concatenate([o_p, o_s.astype(BF16)], axis=0)

        x1, comb_t = _merge(x_all, o_pool, o_gmlp, o_nsa, w_merge[l], w_bp[l], w_bg[l], w_bn[l], w_o[l],
                            ln1_g[l][None, :], ln1_b[l][None, :], w_router_t, b_router_c, alpha)
        x_all = _moe(x1, comb_t.T, wg_b, wu_b, wd_b, l, ln2_g[l][None, :], ln2_b[l][None, :], alpha)

        kv_shape = (2, N_KV_HEADS, HEAD_DIM)
        zkv_s = z_kv[np_rows:].reshape(nb, tq, 3, *kv_shape)
        a_p = z_auv[:np_rows, :pw].reshape(batch, seq, pw)
        a_s = z_auv[np_rows:, :pw].reshape(nb, tq, pw)
        zkv_p = zkv_p.reshape(batch, seq, 3, *kv_shape)
        st_p[0].append(zkv_p[:, :, 0])
        st_p[1].append(zkv_p[:, :, 1])
        st_p[2].append(zkv_p[:, seq - min(WINDOW, seq):, 2])
        st_p[3].append(a_p[:, seq - POOL_KEEP:])
        st_s[0].append(zkv_s[:, :, 0])
        st_s[1].append(zkv_s[:, :, 1])
        st_s[2].append(jnp.concatenate([state_win_kv[l], zkv_s[:, :, 2]], axis=1)[:, tq:])
        st_s[3].append(jnp.concatenate([state_pool[l], a_s], axis=1)[:, tq:])
        st_s[4].append(vn_s.transpose(1, 0, 2))

    y_prompt = x_all[:np_rows].reshape(batch, seq, d)
    y_sample = x_all[np_rows:].reshape(nb, tq, d)
    return (y_prompt, y_sample,
            jnp.stack(st_p[0]), jnp.stack(st_p[1]), jnp.stack(st_p[2]), jnp.stack(st_p[3]),
            jnp.stack(st_s[0]), jnp.stack(st_s[1]), jnp.stack(st_s[2]), jnp.stack(st_s[3]), jnp.stack(st_s[4]))
```

```python
import functools
import math

import numpy as np
import jax
import jax.numpy as jnp
from jax import lax
from jax.experimental import pallas as pl
from jax.experimental.pallas import tpu as pltpu

F32 = jnp.float32
BF16 = jnp.bfloat16

POOL_WINDOWS = (2, 4, 8, 16)
POOL_KEEP = max(POOL_WINDOWS) - 1
GMLP_GROUPS = 4
CHUNK = 128
N_HEADS = 8
HEAD_DIM = 64
N_KV_HEADS = 2
GQA = N_HEADS // N_KV_HEADS
ROT_DIM = HEAD_DIM // 4
ROPE_THETA = 500000.0
CMP_LEN = 32
CMP_STRIDE = 16
CMP_HIDDEN = 2 * HEAD_DIM
SEL_LEN = 64
SEL_TOPN = 16
WINDOW = 512
Q_BLOCK = 128
N_EXPERTS = 16
EXPERTS_PER_GROUP = 4
LN_EPS = 1e-5
NEG_INF = -1e30
FORCE = 1e9

LANES = 128
KEY_CHUNK = 512
VMEM_LIMIT = 56 * 1024 * 1024


def _cparams(*sem):
    return pltpu.CompilerParams(dimension_semantics=sem, vmem_limit_bytes=VMEM_LIMIT)


def _tile(n, cap):
    t = cap
    while n % t:
        t //= 2
    return t


def _dot(a, b):
    return jnp.dot(a, b, preferred_element_type=F32)


def _dot_nt(a, b):
    return lax.dot_general(a, b, (((1,), (1,)), ((), ())), preferred_element_type=F32)


def _layer_norm(x, g, b):
    mu = jnp.mean(x, -1, keepdims=True)
    xc = x - mu
    var = jnp.mean(xc * xc, -1, keepdims=True)
    return xc * lax.rsqrt(var + LN_EPS) * g + b


def _masked_softmax(s, mask):
    sm = jnp.where(mask, s, NEG_INF)
    p = jnp.where(mask, jnp.exp(sm - jnp.max(sm, -1, keepdims=True)), 0.0)
    d = jnp.sum(p, -1, keepdims=True)
    return p / jnp.where(d > 0, d, 1.0)


def _chunked_attention(q, kt_ref, v_ref, bias_fn, c_lo, c_hi, mx_sc, acc_sc):
    rows, half = mx_sc.shape
    reps = rows // Q_BLOCK

    def scores(c):
        return _dot(q, kt_ref[c]) + jnp.concatenate([bias_fn(c)] * reps, axis=0)

    mx_sc[...] = jnp.full(mx_sc.shape, -jnp.inf, F32)

    @pl.loop(c_lo, c_hi)
    def _(c):
        s = scores(c)
        mx_sc[...] = jnp.maximum(mx_sc[...], jnp.maximum(s[:, :half], s[:, half:]))

    m = jnp.max(mx_sc[...], -1, keepdims=True)
    acc_sc[...] = jnp.zeros(acc_sc.shape, F32)

    @pl.loop(c_lo, c_hi)
    def _(c):
        p = jnp.exp(scores(c) - m).astype(BF16)
        acc_sc[...] += _dot(p, v_ref[c])

    acc = acc_sc[...]
    return acc / pltpu.roll(acc, acc.shape[1] // 2, 1)


def _topk_rows(score_t, k, n_valid):
    nq = score_t.shape[1]
    sub_rows = 8
    nv = -(-n_valid // sub_rows)
    tiles = [score_t[sub_rows * v:sub_rows * (v + 1)] for v in range(nv)]
    sub = lax.broadcasted_iota(jnp.int32, (sub_rows, nq), 0)
    ranks = [jnp.zeros((sub_rows, nq), F32) for _ in range(nv)]
    for i in range(n_valid):
        vi, si = divmod(i, sub_rows)
        r = jnp.broadcast_to(score_t[i:i + 1], (sub_rows, nq))
        for v in range(nv):
            if v < vi:
                ahead = r > tiles[v]
            elif v > vi:
                ahead = r >= tiles[v]
            else:
                ahead = (r > tiles[v]) | ((r >= tiles[v]) & (sub > si))
            ranks[v] = ranks[v] + jnp.where(ahead, 1.0, 0.0)
    sel = [jnp.where((ranks[v] < k) & (sub_rows * v + sub < n_valid), 1.0, 0.0) for v in range(nv)]
    pad = score_t.shape[0] - sub_rows * nv
    if pad:
        sel.append(jnp.zeros((pad, nq), F32))
    return jnp.concatenate(sel, axis=0)


def _importance(p_cmp, rows, ovl):
    psum = p_cmp[0:rows]
    for g in range(1, GQA):
        psum = psum + p_cmp[g * rows:(g + 1) * rows]
    hi = psum.astype(BF16)
    lo = (psum - hi.astype(F32)).astype(BF16)
    return _dot(hi, ovl) + _dot(lo, ovl)


def _select_blocks(imp, qpos_t, nsb):
    imp_t = imp.T
    blk = lax.broadcasted_iota(jnp.int32, imp_t.shape, 0)
    cur = jnp.right_shift(qpos_t, int(math.log2(SEL_LEN)))
    forced = (blk == 0) | (blk == cur) | (blk == cur - 1)
    score_t = jnp.where(forced, FORCE, jnp.where(blk <= cur, imp_t, -FORCE))
    return _topk_rows(score_t, min(SEL_TOPN, nsb), nsb).T


def _head_gate(gs, col):
    lane = lax.broadcasted_iota(jnp.int32, gs.shape, 1)
    return jnp.sum(jnp.where(lane == col, gs, 0.0), -1, keepdims=True)


def _proj_kernel(x_ref, w_ref, wg_ref, cos_ref, sm_ref, sp_ref, auv_ref, q_ref, kv_ref, g_ref, *, n_auv, n_q, n_kv):
    xb = x_ref[...].astype(BF16)
    z = _dot(xb, w_ref[...])
    auv_ref[...] = z[:, :n_auv]
    cos, sm, sp = cos_ref[...], sm_ref[...], sp_ref[...]
    half = ROT_DIM // 2

    def rope(t):
        return t * cos + pltpu.roll(t, LANES - half, 1) * sm + pltpu.roll(t, half, 1) * sp

    for j in range(n_q // LANES):
        c0 = n_auv + j * LANES
        q_ref[:, j * LANES:(j + 1) * LANES] = rope(z[:, c0:c0 + LANES]) * (HEAD_DIM ** -0.5)
    for j in range(n_kv // LANES):
        c0 = n_auv + n_q + j * LANES
        t = z[:, c0:c0 + LANES]
        kv_ref[:, j * LANES:(j + 1) * LANES] = rope(t) if j % 2 == 0 else t
    g_ref[...] = _dot(xb, wg_ref[...])


def _proj(x, w_main, w_gate, cos, sm, sp, n_auv, n_q, n_kv):
    n, d = x.shape
    tm = _tile(n, 512)
    row = lambda i: (i, 0)
    fixed = lambda i: (0, 0)
    return pl.pallas_call(
        functools.partial(_proj_kernel, n_auv=n_auv, n_q=n_q, n_kv=n_kv),
        grid=(n // tm,),
        in_specs=[pl.BlockSpec((tm, d), row),
                  pl.BlockSpec(w_main.shape, fixed),
                  pl.BlockSpec(w_gate.shape, fixed),
                  pl.BlockSpec((tm, LANES), row), pl.BlockSpec((tm, LANES), row), pl.BlockSpec((tm, LANES), row)],
        out_specs=[pl.BlockSpec((tm, n_auv), row), pl.BlockSpec((tm, n_q), row),
                   pl.BlockSpec((tm, n_kv), row), pl.BlockSpec((tm, LANES), row)],
        out_shape=[jax.ShapeDtypeStruct((n, n_auv), F32), jax.ShapeDtypeStruct((n, n_q), F32),
                   jax.ShapeDtypeStruct((n, n_kv), F32), jax.ShapeDtypeStruct((n, LANES), F32)],
        compiler_params=_cparams("parallel"),
        name="in_proj",
    )(x, w_main, w_gate, cos, sm, sp)


def _pool_select(grp, vals):
    out = vals[-1]
    for gi in range(len(vals) - 2, -1, -1):
        out = jnp.where(grp == gi, vals[gi], out)
    return out


def _mix_prompt_kernel(auv_ref, halo_ref, pw_ref, ps_ref, lng_ref, lnb_ref, ws_ref, bsr_ref, op_ref, og_ref, *, tm, pw):
    i = pl.program_id(1)
    halo_rows = halo_ref.shape[0]
    a = auv_ref[:, 0:pw]
    u = auv_ref[:, pw:2 * pw]
    v = auv_ref[:, 2 * pw:3 * pw]
    halo = jnp.where(i == 0, 0.0, halo_ref[...])
    ext = jnp.concatenate([halo, a], axis=0)
    sums, cur, width = [], ext, 1
    for win in POOL_WINDOWS:
        while width < win:
            cur = cur + pltpu.roll(cur, width, 0)
            width *= 2
        sums.append(cur[halo_rows:])
    lane = lax.broadcasted_iota(jnp.int32, (tm, pw), 1)
    grp = lane // (pw // len(POOL_WINDOWS))
    pos = (i * tm + lax.broadcasted_iota(jnp.int32, (tm, pw), 0)).astype(F32)
    total = _pool_select(grp, sums)
    win = _pool_select(grp, [jnp.full((tm, pw), float(w), F32) for w in POOL_WINDOWS])
    d = total / jnp.minimum(win, pos + 1.0) - a
    op_ref[...] = (_dot(d.astype(BF16), pw_ref[...]) * ps_ref[...]).astype(op_ref.dtype)

    vn = _layer_norm(v, lng_ref[...], lnb_ref[...]).astype(BF16)
    r = lax.broadcasted_iota(jnp.int32, (CHUNK, CHUNK), 0)
    c = lax.broadcasted_iota(jnp.int32, (CHUNK, CHUNK), 1)
    wms = [jnp.where(c <= r, ws_ref[g], 0.0).astype(BF16) for g in range(GMLP_GROUPS)]
    grp_c = lax.broadcasted_iota(jnp.int32, (CHUNK, pw), 1) // (pw // GMLP_GROUPS)
    for j in range(tm // CHUNK):
        rows = slice(j * CHUNK, (j + 1) * CHUNK)
        vc = vn[rows]
        mix = _pool_select(grp_c, [_dot(wms[g], vc) for g in range(GMLP_GROUPS)])
        og_ref[rows, :] = (u[rows] * (mix + bsr_ref[...])).astype(og_ref.dtype)


def _mix_prompt(z_auv, batch, seq, pool_bd, pool_scale, ln_g, ln_b, ws, bs_rows):
    pw = pool_scale.shape[-1]
    tm = _tile(seq, 512)
    nt = seq // tm
    halo = 2 * 8
    hb = tm // halo
    fixed2 = lambda b, i: (0, 0)
    return pl.pallas_call(
        functools.partial(_mix_prompt_kernel, tm=tm, pw=pw),
        grid=(batch, nt),
        in_specs=[pl.BlockSpec((tm, 3 * pw), lambda b, i: (b * nt + i, 0)),
                  pl.BlockSpec((halo, pw), lambda b, i: (jnp.maximum((b * nt + i) * hb - 1, 0), 0)),
                  pl.BlockSpec((pw, pw), fixed2), pl.BlockSpec((1, pw), fixed2),
                  pl.BlockSpec((1, pw), fixed2), pl.BlockSpec((1, pw), fixed2),
                  pl.BlockSpec(ws.shape, lambda b, i: (0, 0, 0)),
                  pl.BlockSpec((CHUNK, pw), fixed2)],
        out_specs=[pl.BlockSpec((tm, pw), lambda b, i: (b * nt + i, 0)),
                   pl.BlockSpec((tm, pw), lambda b, i: (b * nt + i, 0))],
        out_shape=[jax.ShapeDtypeStruct((batch * seq, pw), BF16), jax.ShapeDtypeStruct((batch * seq, pw), BF16)],
        compiler_params=_cparams("parallel", "parallel"),
        name="mix_prompt",
    )(z_auv, z_auv, pool_bd, pool_scale, ln_g, ln_b, ws, bs_rows)


def _mix_sample_kernel(a_ref, st_ref, pw_ref, ps_ref, lng_ref, lnb_ref, wrow_ref, bsr_ref, op_ref, og_ref, vn_ref,
                       *, tq, keep, pos0, pw):
    nb = a_ref.shape[1]
    ext = [st_ref[i] for i in range(keep)] + [a_ref[t][:, 0:pw] for t in range(tq)]
    grp = lax.broadcasted_iota(jnp.int32, (nb, pw), 1) // (pw // len(POOL_WINDOWS))
    for t in range(tq):
        e = keep + t
        acc, k, sums = ext[e], 1, []
        for win in POOL_WINDOWS:
            while k < win:
                if e - k >= 0:
                    acc = acc + ext[e - k]
                k += 1
            sums.append(acc)
        total = _pool_select(grp, sums)
        cnt = _pool_select(grp, [jnp.full((nb, pw), min(float(w), pos0 + t + 1.0), F32) for w in POOL_WINDOWS])
        d = total / cnt - ext[e]
        op_ref[t] = (_dot(d.astype(BF16), pw_ref[...]) * ps_ref[...]).astype(op_ref.dtype)
    vns = []
    for t in range(tq):
        vn = _layer_norm(a_ref[t][:, 2 * pw:3 * pw], lng_ref[...], lnb_ref[...])
        vn_ref[t] = vn
        vns.append(vn)
    for t in range(tq):
        mix = bsr_ref[t:t + 1, :]
        for s in range(t + 1):
            mix = mix + wrow_ref[t * tq + s:t * tq + s + 1, :] * vns[s]
        og_ref[t] = (a_ref[t][:, pw:2 * pw] * mix).astype(og_ref.dtype)


def _mix_sample(a_t, st_t, pos0, pool_bd, pool_scale, ln_g, ln_b, wrow, bsrow):
    tq, nb, w3 = a_t.shape
    keep = st_t.shape[0]
    pw = w3 // 3
    full = lambda shape: pl.BlockSpec(shape, lambda i: (0,) * len(shape))
    return pl.pallas_call(
        functools.partial(_mix_sample_kernel, tq=tq, keep=keep, pos0=pos0, pw=pw),
        grid=(1,),
        in_specs=[full(a_t.shape), full(st_t.shape), full(pool_bd.shape), full(pool_scale.shape),
                  full(ln_g.shape), full(ln_b.shape), full(wrow.shape), full(bsrow.shape)],
        out_specs=[full((tq, nb, pw)), full((tq, nb, pw)), full((tq, nb, pw))],
        out_shape=[jax.ShapeDtypeStruct((tq, nb, pw), BF16), jax.ShapeDtypeStruct((tq, nb, pw), BF16),
                   jax.ShapeDtypeStruct((tq, nb, pw), F32)],
        compiler_params=_cparams("arbitrary"),
        name="mix_sample",
    )(a_t, st_t, pool_bd, pool_scale, ln_g, ln_b, wrow, bsrow)


def _cmp1_kernel(x_ref, w_ref, pe_ref, h_ref, *, g):
    hw = N_KV_HEADS * CMP_HIDDEN
    xcat = jnp.concatenate([x_ref[pl.ds(r, g, stride=CMP_STRIDE), :] for r in range(CMP_STRIDE)], axis=1)
    for ab in range(2):
        lhs = (xcat + pe_ref[ab]).astype(BF16)
        h_ref[:, ab * hw:(ab + 1) * hw] = _dot(lhs, w_ref[ab])


def _cmp1(x2d, row_block0, n_rows, w1_bd, pe_rows):
    tr = _tile(n_rows, 4096)
    g = tr // CMP_STRIDE
    kvw = N_KV_HEADS * HEAD_DIM
    hcols = 2 * N_KV_HEADS * CMP_HIDDEN
    return pl.pallas_call(
        functools.partial(_cmp1_kernel, g=g),
        grid=(n_rows // tr, 2),
        in_specs=[pl.BlockSpec((tr, kvw), lambda i, kv: (row_block0(tr) + i, kv)),
                  pl.BlockSpec((None,) + w1_bd.shape[1:], lambda i, kv: (kv, 0, 0, 0)),
                  pl.BlockSpec((None,) + pe_rows.shape[1:], lambda i, kv: (kv, 0, 0, 0))],
        out_specs=pl.BlockSpec((g, hcols), lambda i, kv: (i, kv)),
        out_shape=jax.ShapeDtypeStruct((n_rows // CMP_STRIDE, 2 * hcols), F32),
        compiler_params=_cparams("parallel", "parallel"),
        name="cmp_stage1",
    )(x2d, w1_bd, pe_rows)


def _cmp1_pages_kernel(xt_ref, w_ref, pe_ref, h_ref, x_sc, *, g):
    n_pages, n_h, hd, page = xt_ref.shape
    for p in range(n_pages):
        x_sc[p * page:(p + 1) * page, :] = xt_ref[p].reshape(n_h * hd, page).T
    _cmp1_kernel(x_sc, w_ref, pe_ref, h_ref, g=g)


def _cmp1_pages(cache_t, layer, w1_bd, pe_rows):
    _, n_pool, _, n_h, hd, page = cache_t.shape
    pp = _tile(n_pool, 4096 // page)
    g = pp * page // CMP_STRIDE
    hcols = 2 * N_KV_HEADS * CMP_HIDDEN
    return pl.pallas_call(
        functools.partial(_cmp1_pages_kernel, g=g),
        grid=(n_pool // pp, 2),
        in_specs=[pl.BlockSpec((None, pp, None, n_h, hd, page), lambda i, kv: (layer, i, kv, 0, 0, 0)),
                  pl.BlockSpec((None,) + w1_bd.shape[1:], lambda i, kv: (kv, 0, 0, 0)),
                  pl.BlockSpec((None,) + pe_rows.shape[1:], lambda i, kv: (kv, 0, 0, 0))],
        out_specs=pl.BlockSpec((g, hcols), lambda i, kv: (i, kv)),
        out_shape=jax.ShapeDtypeStruct((n_pool * page // CMP_STRIDE, 2 * hcols), F32),
        scratch_shapes=[pltpu.VMEM((pp * page, n_h * hd), F32)],
        compiler_params=_cparams("parallel", "parallel"),
        name="cmp_stage1_pages",
    )(cache_t, w1_bd, pe_rows)


def _gelu_tanh(x):
    return 0.5 * x * (1.0 + jnp.tanh(math.sqrt(2.0 / math.pi) * (x + 0.044715 * (x * x * x))))


def _cmp2_body(h_refs, b1_ref, w2kt_ref, w2v_ref, kct_ref, vc_ref):
    hall = jnp.concatenate([r[...] for r in h_refs], axis=0) if len(h_refs) > 1 else h_refs[0][...]
    g = hall.shape[0]
    hw = N_KV_HEADS * CMP_HIDDEN
    for kv in range(2):
        for h in range(N_KV_HEADS):
            ca = (kv * 2) * hw + h * CMP_HIDDEN
            cb = (kv * 2 + 1) * hw + h * CMP_HIDDEN
            hid = hall[:, ca:ca + CMP_HIDDEN] + pltpu.roll(hall[:, cb:cb + CMP_HIDDEN], g - 1, 0) + b1_ref[kv]
            hid = _gelu_tanh(hid).astype(BF16)
            if kv == 0:
                kct_ref[h] = _dot_nt(w2kt_ref[...], hid).astype(kct_ref.dtype)
            else:
                vc_ref[h] = _dot(hid, w2v_ref[...]).astype(vc_ref.dtype)


def _cmp2_prompt_kernel(h_ref, b1_ref, w2kt_ref, w2v_ref, kct_ref, vc_ref):
    _cmp2_body([h_ref], b1_ref, w2kt_ref, w2v_ref, kct_ref, vc_ref)


def _cmp2_sample_kernel(pt_ref, *refs, npages):
    _cmp2_body(list(refs[:npages]), *refs[npages:])


def _cmp2_out(n, g):
    specs = [pl.BlockSpec((None, N_KV_HEADS, HEAD_DIM, g), lambda b, *_: (b, 0, 0, 0)),
             pl.BlockSpec((None, N_KV_HEADS, g, HEAD_DIM), lambda b, *_: (b, 0, 0, 0))]
    shapes = [jax.ShapeDtypeStruct((n, N_KV_HEADS, HEAD_DIM, g), BF16),
              jax.ShapeDtypeStruct((n, N_KV_HEADS, g, HEAD_DIM), BF16)]
    return specs, shapes


def _cmp2_prompt(h1, b1, w2kt, w2v):
    n, g, hc = h1.shape
    out_specs, out_shape = _cmp2_out(n, g)
    return pl.pallas_call(
        _cmp2_prompt_kernel,
        grid=(n,),
        in_specs=[pl.BlockSpec((None, g, hc), lambda b: (b, 0, 0)),
                  pl.BlockSpec(b1.shape, lambda b: (0, 0, 0)),
                  pl.BlockSpec(w2kt.shape, lambda b: (0, 0)),
                  pl.BlockSpec(w2v.shape, lambda b: (0, 0))],
        out_specs=out_specs, out_shape=out_shape,
        compiler_params=_cparams("parallel"),
        name="cmp_stage2_prompt",
    )(h1, b1, w2kt, w2v)


def _cmp2_sample(page_table, h1_pool, b1, w2kt, w2v):
    n, npages = page_table.shape
    _, gp, hc = h1_pool.shape
    out_specs, out_shape = _cmp2_out(n, npages * gp)
    page_specs = [pl.BlockSpec((None, gp, hc), functools.partial(lambda b, pt, j: (pt[b, j], 0, 0), j=j))
                  for j in range(npages)]
    return pl.pallas_call(
        functools.partial(_cmp2_sample_kernel, npages=npages),
        grid_spec=pltpu.PrefetchScalarGridSpec(
            num_scalar_prefetch=1, grid=(n,),
            in_specs=page_specs + [pl.BlockSpec(b1.shape, lambda b, pt: (0, 0, 0)),
                                   pl.BlockSpec(w2kt.shape, lambda b, pt: (0, 0)),
                                   pl.BlockSpec(w2v.shape, lambda b, pt: (0, 0))],
            out_specs=out_specs),
        out_shape=out_shape,
        compiler_params=_cparams("parallel"),
        name="cmp_stage2_sample",
    )(page_table, *([h1_pool] * npages), b1, w2kt, w2v)


def _nsa_prompt_kernel(q_ref, g_ref, kct_ref, vc_ref, kts_ref, vs_ref, ktw_ref, vw_ref, ovl_ref, exp_ref, o_ref,
                       mx_sc, acc_sc, *, nsb):
    h = pl.program_id(1)
    qb = pl.program_id(2)
    tq = Q_BLOCK
    rows = GQA * tq
    t0 = qb * tq
    qf = q_ref[...]
    q = jnp.concatenate([qf[:, g * HEAD_DIM:(g + 1) * HEAD_DIM] for g in range(GQA)], axis=0).astype(BF16)
    qpos_r = t0 + (lax.broadcasted_iota(jnp.int32, (rows, 1), 0) & (tq - 1))

    ncp = kct_ref.shape[-1]
    s = _dot(q, kct_ref[...])
    cend = lax.broadcasted_iota(jnp.int32, (1, ncp), 1) * CMP_STRIDE + (CMP_LEN - 1)
    p_cmp = _masked_softmax(s, cend <= qpos_r)
    o_cmp = _dot(p_cmp.astype(BF16), vc_ref[...])

    imp = _importance(p_cmp, tq, ovl_ref[...])
    qpos_t = t0 + lax.broadcasted_iota(jnp.int32, (imp.shape[1], imp.shape[0]), 1)
    selm = _select_blocks(imp, qpos_t, nsb).astype(BF16)
    qpos_c = t0 + lax.broadcasted_iota(jnp.int32, (tq, 1), 0)
    kcol = lax.broadcasted_iota(jnp.int32, (1, KEY_CHUNK), 1)

    def sel_bias(c):
        e = _dot(selm, exp_ref[c])
        return jnp.where((e > 0.5) & (c * KEY_CHUNK + kcol <= qpos_c), 0.0, NEG_INF)

    def win_bias(c):
        kpos = c * KEY_CHUNK + kcol
        return jnp.where((kpos <= qpos_c) & (kpos > qpos_c - WINDOW), 0.0, NEG_INF)

    c_hi = (t0 + tq - 1) // KEY_CHUNK + 1
    c_lo = jnp.maximum(t0 - WINDOW + 1, 0) // KEY_CHUNK
    o_sel = _chunked_attention(q, kts_ref, vs_ref, sel_bias, 0, c_hi, mx_sc, acc_sc)
    o_win = _chunked_attention(q, ktw_ref, vw_ref, win_bias, c_lo, c_hi, mx_sc, acc_sc)

    gs = jax.nn.sigmoid(g_ref[...])
    outs = []
    for g in range(GQA):
        r = slice(g * tq, (g + 1) * tq)
        col = (h * GQA + g) * 3
        outs.append(_head_gate(gs, col) * o_cmp[r] + _head_gate(gs, col + 1) * o_sel[r, :HEAD_DIM]
                    + _head_gate(gs, col + 2) * o_win[r, :HEAD_DIM])
    o_ref[...] = jnp.concatenate(outs, axis=1).astype(o_ref.dtype)


def _nsa_prompt(z_q, z_g, kct, vc, kts, vs, ktw, vw, ovl, expand, batch, seq):
    nqb = seq // Q_BLOCK
    hq = GQA * HEAD_DIM
    nsb = -(-seq // SEL_LEN)
    per_head = lambda shape: pl.BlockSpec((None, None) + shape, lambda b, h, i: (b, h) + (0,) * len(shape))
    return pl.pallas_call(
        functools.partial(_nsa_prompt_kernel, nsb=nsb),
        grid=(batch, N_KV_HEADS, nqb),
        in_specs=[pl.BlockSpec((Q_BLOCK, hq), lambda b, h, i: (b * nqb + i, h)),
                  pl.BlockSpec((Q_BLOCK, LANES), lambda b, h, i: (b * nqb + i, 0)),
                  per_head(kct.shape[2:]), per_head(vc.shape[2:]),
                  per_head(kts.shape[2:]), per_head(vs.shape[2:]),
                  per_head(ktw.shape[2:]), per_head(vw.shape[2:]),
                  pl.BlockSpec(ovl.shape, lambda b, h, i: (0, 0)),
                  pl.BlockSpec(expand.shape, lambda b, h, i: (0, 0, 0))],
        out_specs=pl.BlockSpec((Q_BLOCK, hq), lambda b, h, i: (b * nqb + i, h)),
        out_shape=jax.ShapeDtypeStruct((batch * seq, N_KV_HEADS * hq), BF16),
        scratch_shapes=[pltpu.VMEM((GQA * Q_BLOCK, KEY_CHUNK // 2), F32), pltpu.VMEM((GQA * Q_BLOCK, vs.shape[-1]), F32)],
        compiler_params=_cparams("parallel", "parallel", "parallel"),
        name="nsa_prompt",
    )(z_q, z_g, kct, vc, kts, vs, ktw, vw, ovl, expand)


def _nsa_sample_kernel(pt_ref, q_ref, g_ref, kvn_ref, kct_ref, vc_ref, ovl_ref, exp_ref, win_ref, *rest,
                       npages, tq, pos0, nsb):
    page_refs = rest[:npages]
    o_ref = rest[npages]
    newpg = rest[npages + 1]
    page = newpg.shape[0]
    kvw = N_KV_HEADS * HEAD_DIM
    newpg[...] = jnp.zeros(newpg.shape, newpg.dtype)
    newpg[0:tq, :] = kvn_ref[...]
    rows = GQA * tq
    qpos_r = pos0 + lax.rem(lax.broadcasted_iota(jnp.int32, (rows, 1), 0), tq)
    qf = q_ref[...]
    gs = jax.nn.sigmoid(g_ref[...])
    keep = win_ref.shape[-1]
    heads = range(N_KV_HEADS)
    qs = [jnp.concatenate([qf[:, (h * GQA + g) * HEAD_DIM:(h * GQA + g + 1) * HEAD_DIM] for g in range(GQA)],
                          axis=0).astype(BF16) for h in heads]

    ncp = kct_ref.shape[-1]
    cend = lax.broadcasted_iota(jnp.int32, (1, ncp), 1) * CMP_STRIDE + (CMP_LEN - 1)
    p_cmps = [_masked_softmax(_dot(qs[h], kct_ref[h]), cend <= qpos_r) for h in heads]
    o_cmps = [_dot(p_cmps[h].astype(BF16), vc_ref[h]) for h in heads]

    imps = [_importance(p_cmps[h], tq, ovl_ref[...]) for h in heads]
    imp = jnp.concatenate(imps + [jnp.zeros((LANES - N_KV_HEADS * tq, LANES), F32)], axis=0)
    qpos_t = pos0 + lax.rem(lax.broadcasted_iota(jnp.int32, (LANES, LANES), 1), tq)
    selm_all = _select_blocks(imp, qpos_t, nsb)

    outs = []
    for h in heads:
        q, o_cmp = qs[h], o_cmps[h]
        kc0 = h * HEAD_DIM
        vc0 = kvw + h * HEAD_DIM

        selm = selm_all[h * tq:(h + 1) * tq]
        e = _dot(jnp.concatenate([selm] * GQA, axis=0).astype(BF16), exp_ref[...])
        k_new = newpg[:, 2 * kvw + kc0:2 * kvw + kc0 + HEAD_DIM].astype(BF16)
        v_new = newpg[:, 2 * kvw + vc0:2 * kvw + vc0 + HEAD_DIM].astype(BF16)
        s = jnp.concatenate([_dot(q, page_refs[p][0, h].astype(BF16)) for p in range(npages)] + [_dot_nt(q, k_new)], axis=1)
        kpos = lax.broadcasted_iota(jnp.int32, (1, s.shape[1]), 1)
        p = _masked_softmax(s, (e > 0.5) & (kpos <= qpos_r)).astype(BF16)
        o_sel = _dot(p[:, npages * page:], v_new)
        for j in range(npages):
            o_sel = o_sel + _dot_nt(p[:, j * page:(j + 1) * page], page_refs[j][1, h].astype(BF16))

        k_new = newpg[:, 4 * kvw + kc0:4 * kvw + kc0 + HEAD_DIM].astype(BF16)
        v_new = newpg[:, 4 * kvw + vc0:4 * kvw + vc0 + HEAD_DIM].astype(BF16)
        s = jnp.concatenate([_dot(q, win_ref[0, h].astype(BF16)), _dot_nt(q, k_new)], axis=1)
        kpos = pos0 - keep + lax.broadcasted_iota(jnp.int32, (1, s.shape[1]), 1)
        p = _masked_softmax(s, (kpos <= qpos_r) & (kpos > qpos_r - WINDOW)).astype(BF16)
        o_win = _dot_nt(p[:, :keep], win_ref[1, h].astype(BF16)) + _dot(p[:, keep:], v_new)

        for g in range(GQA):
            r = slice(g * tq, (g + 1) * tq)
            col = (h * GQA + g) * 3
            outs.append(_head_gate(gs, col) * o_cmp[r] + _head_gate(gs, col + 1) * o_sel[r]
                        + _head_gate(gs, col + 2) * o_win[r])
    o_ref[...] = jnp.concatenate(outs, axis=1).astype(o_ref.dtype)


def _nsa_sample(page_table, z_q, z_g, z_kv, row0, kct, vc, ovl, expand, cache_sel, state_win, layer, tq, pos0, nsb):
    n, npages = page_table.shape
    page = cache_sel.shape[-1]
    blk0 = row0 // tq
    hq = z_q.shape[1]
    page_specs = [pl.BlockSpec((None, None) + cache_sel.shape[2:],
                               functools.partial(lambda b, pt, j: (layer, pt[b, j], 0, 0, 0, 0), j=j))
                  for j in range(npages)]
    return pl.pallas_call(
        functools.partial(_nsa_sample_kernel, npages=npages, tq=tq, pos0=pos0, nsb=nsb),
        grid_spec=pltpu.PrefetchScalarGridSpec(
            num_scalar_prefetch=1, grid=(n,),
            in_specs=[pl.BlockSpec((tq, hq), lambda b, pt: (blk0 + b, 0)),
                      pl.BlockSpec((tq, LANES), lambda b, pt: (blk0 + b, 0)),
                      pl.BlockSpec((tq, z_kv.shape[1]), lambda b, pt: (blk0 + b, 0)),
                      pl.BlockSpec((None,) + kct.shape[1:], lambda b, pt: (b, 0, 0, 0)),
                      pl.BlockSpec((None,) + vc.shape[1:], lambda b, pt: (b, 0, 0, 0)),
                      pl.BlockSpec(ovl.shape, lambda b, pt: (0, 0)),
                      pl.BlockSpec(expand.shape, lambda b, pt: (0, 0)),
                      pl.BlockSpec((None, None) + state_win.shape[2:], lambda b, pt: (layer, b, 0, 0, 0, 0))] + page_specs,
            out_specs=pl.BlockSpec((tq, hq), lambda b, pt: (b, 0)),
            scratch_shapes=[pltpu.VMEM((page, z_kv.shape[1]), F32)]),
        out_shape=jax.ShapeDtypeStruct((n * tq, hq), F32),
        compiler_params=_cparams("parallel"),
        name="nsa_sample",
    )(page_table, z_q, z_g, z_kv, kct, vc, ovl, expand, state_win, *([cache_sel] * npages))


def _route(x1, wrt, br, ct_ref):
    logits = lax.dot_general(wrt, x1, (((1,), (1,)), ((), ())), precision=lax.Precision.HIGHEST,
                             preferred_element_type=F32)
    aff = jax.nn.sigmoid(logits)
    score = aff + br
    sc = [score[e:e + 1, :] for e in range(N_EXPERTS)]
    af = [aff[e:e + 1, :] for e in range(N_EXPERTS)]
    n_grp = N_EXPERTS // EXPERTS_PER_GROUP
    gsum = []
    for k in range(n_grp):
        m = sc[k * EXPERTS_PER_GROUP:(k + 1) * EXPERTS_PER_GROUP]
        best = None
        for i in range(EXPERTS_PER_GROUP):
            for j in range(i + 1, EXPERTS_PER_GROUP):
                pair = m[i] + m[j]
                best = pair if best is None else jnp.maximum(best, pair)
        gsum.append(best)
    is_best, taken = [], None
    for k in range(n_grp):
        ok = None
        for j in range(k + 1, n_grp):
            c = gsum[k] >= gsum[j]
            ok = c if ok is None else ok & c
        if ok is None:
            ok = jnp.ones_like(gsum[k], dtype=jnp.bool_)
        if taken is not None:
            ok = ok & jnp.logical_not(taken)
        is_best.append(ok)
        taken = ok if taken is None else taken | ok
    sel = []
    for e in range(N_EXPERTS):
        k = e // EXPERTS_PER_GROUP
        rank = jnp.zeros_like(sc[e])
        for e2 in range(k * EXPERTS_PER_GROUP, (k + 1) * EXPERTS_PER_GROUP):
            if e2 == e:
                continue
            ahead = (sc[e2] >= sc[e]) if e2 < e else (sc[e2] > sc[e])
            rank = rank + ahead.astype(F32)
        sel.append(is_best[k] & (rank < 2.0))
    denom = jnp.zeros_like(sc[0])
    for e in range(N_EXPERTS):
        denom = denom + jnp.where(sel[e], af[e], 0.0)
    for e in range(N_EXPERTS):
        ct_ref[e:e + 1, :] = jnp.where(sel[e], af[e] / denom, 0.0)


def _merge_kernel(x_ref, op_ref, og_ref, on_ref, wm_ref, wp_ref, wgm_ref, wn_ref, wo_ref, lg_ref, lb_ref,
                  wrt_ref, br_ref, x1_ref, ct_ref, *, alpha):
    x = x_ref[...]
    xb = x.astype(BF16)
    d = x.shape[1]
    merged = None
    for b, (o_r, w_r) in enumerate(((op_ref, wp_ref), (og_ref, wgm_ref), (on_ref, wn_ref))):
        gate = jax.nn.sigmoid(_dot(xb, wm_ref[:, b * d:(b + 1) * d]))
        term = gate * _dot(o_r[...], w_r[...])
        merged = term if merged is None else merged + term
    out = _dot(merged.astype(BF16), wo_ref[...])
    x1 = _layer_norm(alpha * x + out, lg_ref[...], lb_ref[...])
    x1_ref[...] = x1
    _route(x1, wrt_ref[...], br_ref[...], ct_ref)


def _merge(x, o_pool, o_gmlp, o_nsa, w_merge, w_bp, w_bg, w_bn, w_out, ln_g, ln_b, w_router_t, b_router, alpha):
    n, d = x.shape
    tm = _tile(n, 512)
    row = lambda i: (i, 0)
    fixed = lambda a: pl.BlockSpec(a.shape, lambda i: (0, 0))
    return pl.pallas_call(
        functools.partial(_merge_kernel, alpha=alpha),
        grid=(n // tm,),
        in_specs=[pl.BlockSpec((tm, d), row), pl.BlockSpec((tm, o_pool.shape[1]), row),
                  pl.BlockSpec((tm, o_gmlp.shape[1]), row), pl.BlockSpec((tm, o_nsa.shape[1]), row),
                  fixed(w_merge), fixed(w_bp), fixed(w_bg), fixed(w_bn), fixed(w_out), fixed(ln_g), fixed(ln_b),
                  fixed(w_router_t), fixed(b_router)],
        out_specs=[pl.BlockSpec((tm, d), row), pl.BlockSpec((N_EXPERTS, tm), lambda i: (0, i))],
        out_shape=[jax.ShapeDtypeStruct((n, d), F32), jax.ShapeDtypeStruct((N_EXPERTS, n), F32)],
        compiler_params=_cparams("parallel"),
        name="merge_ln_route",
    )(x, o_pool, o_gmlp, o_nsa, w_merge, w_bp, w_bg, w_bn, w_out, ln_g, ln_b, w_router_t, b_router)


def _moe_kernel(x_ref, c_ref, wg_ref, wu_ref, wd_ref, lg_ref, lb_ref, y_ref, xb_s, acc_s, *, alpha):
    e = pl.program_id(1)

    @pl.when(e == 0)
    def _():
        xb_s[...] = x_ref[...].astype(BF16)
        acc_s[...] = jnp.zeros(acc_s.shape, acc_s.dtype)

    xb = xb_s[...]
    hg = _dot(xb, wg_ref[...])
    hu = _dot(xb, wu_ref[...])
    comb = c_ref[...]
    lane = lax.broadcasted_iota(jnp.int32, comb.shape, 1)
    ce = jnp.sum(jnp.where(lane == e, comb, 0.0), -1, keepdims=True)
    act = (hg * jax.nn.sigmoid(hg)) * hu * ce
    acc_s[...] += _dot(act.astype(BF16), wd_ref[...])

    @pl.when(e == pl.num_programs(1) - 1)
    def _():
        y_ref[...] = _layer_norm(alpha * x_ref[...] + acc_s[...], lg_ref[...], lb_ref[...])


def _moe(x, comb, w_gate, w_up, w_down, layer, ln_g, ln_b, alpha):
    n, d = x.shape
    tm = _tile(n, 512)
    ne, _, de = w_gate.shape[1:]
    return pl.pallas_call(
        functools.partial(_moe_kernel, alpha=alpha),
        grid=(n // tm, ne),
        in_specs=[pl.BlockSpec((tm, d), lambda i, e: (i, 0)),
                  pl.BlockSpec((tm, ne), lambda i, e: (i, 0)),
                  pl.BlockSpec((None, None, d, de), lambda i, e: (layer, e, 0, 0)),
                  pl.BlockSpec((None, None, d, de), lambda i, e: (layer, e, 0, 0)),
                  pl.BlockSpec((None, None, de, d), lambda i, e: (layer, e, 0, 0)),
                  pl.BlockSpec(ln_g.shape, lambda i, e: (0, 0)),
                  pl.BlockSpec(ln_b.shape, lambda i, e: (0, 0))],
        out_specs=pl.BlockSpec((tm, d), lambda i, e: (i, 0)),
        out_shape=jax.ShapeDtypeStruct((n, d), F32),
        scratch_shapes=[pltpu.VMEM((tm, d), BF16), pltpu.VMEM((tm, d), F32)],
        compiler_params=_cparams("parallel", "arbitrary"),
        name="moe_ln",
    )(x, comb, w_gate, w_up, w_down, ln_g, ln_b)


def _rope_tables(pos):
    half = ROT_DIM // 2
    inv = 1.0 / (ROPE_THETA ** (jnp.arange(half, dtype=F32) / half))
    ang = pos.astype(F32)[:, None] * inv[None, :]
    cos, sin = jnp.cos(ang), jnp.sin(ang)
    n = pos.shape[0]
    one = jnp.ones((n, HEAD_DIM - ROT_DIM), F32)
    zero = jnp.zeros((n, HEAD_DIM - ROT_DIM), F32)
    zh = jnp.zeros((n, half), F32)
    rep = LANES // HEAD_DIM
    cos_t = jnp.tile(jnp.concatenate([cos, cos, one], 1), (1, rep))
    sm_t = jnp.tile(jnp.concatenate([-sin, zh, zero], 1), (1, rep))
    sp_t = jnp.tile(jnp.concatenate([zh, sin, zero], 1), (1, rep))
    return cos_t, sm_t, sp_t


def _overlap(nc, nsb, rows):
    i = np.arange(nc)[:, None]
    j = np.arange(nsb)[None, :]
    lo = np.maximum(i * CMP_STRIDE, j * SEL_LEN)
    hi = np.minimum(i * CMP_STRIDE + CMP_LEN, (j + 1) * SEL_LEN)
    ovl = np.zeros((rows, LANES), np.float32)
    ovl[:nc, :nsb] = np.maximum(hi - lo, 0) // CMP_STRIDE
    return jnp.asarray(ovl, BF16)


def _block_diag(blocks):
    n = len(blocks)
    rows = []
    for i, b in enumerate(blocks):
        rows.append(jnp.concatenate([b if j == i else jnp.zeros((b.shape[0], blocks[j].shape[1]), b.dtype)
                                     for j in range(n)], axis=1))
    return jnp.concatenate(rows, axis=0)


def _cmp_weights(w1, pe):
    w_out, pe_out = [], []
    for kv in range(2):
        w_ab, pe_ab = [], []
        for ab in range(2):
            blocks, pes = [], []
            for r in range(CMP_STRIDE):
                rr = ab * CMP_STRIDE + r
                w_r = w1[kv, rr * HEAD_DIM:(rr + 1) * HEAD_DIM, :]
                blocks.append(_block_diag([w_r] * N_KV_HEADS))
                pes.append(jnp.tile(pe[kv, rr], N_KV_HEADS))
            w_ab.append(jnp.concatenate(blocks, axis=0))
            pe_ab.append(jnp.concatenate(pes)[None, :])
        w_out.append(jnp.stack(w_ab))
        pe_out.append(jnp.stack(pe_ab))
    return jnp.stack(w_out).astype(BF16), jnp.stack(pe_out).astype(F32)


def _chunked_kt(k, nck):
    b, t, h, d = k.shape
    return k.transpose(0, 2, 3, 1).reshape(b, h, d, nck, t // nck).transpose(0, 1, 3, 2, 4).astype(BF16)


def _chunked_v(v, nck):
    b, t, h, d = v.shape
    v = v.transpose(0, 2, 1, 3).reshape(b, h, nck, t // nck, d).astype(BF16)
    return jnp.concatenate([v, jnp.ones_like(v)], axis=-1)


def kernel(x_prompt, x_sample, cache_cmp_kv, cache_sel_kv, state_win_kv, state_pool, page_table, w_in, pool_w, pool_scale, gmlp_ln_g, gmlp_ln_b, gmlp_ws, gmlp_bs, cmp_pe, cmp_w1, cmp_b1, cmp_w2, w_branch_pool, w_branch_gmlp, w_branch_nsa, w_out, ln1_g, ln1_b, ln2_g, ln2_b, w_router, b_router, w_gate, w_up, w_down):
    batch, seq, d = x_prompt.shape
    nb, tq, _ = x_sample.shape
    depth = w_in.shape[0]
    n_pool, page = cache_cmp_kv.shape[1:3]
    npages = page_table.shape[1]
    past = npages * page
    keep = state_win_kv.shape[2]
    pw = pool_scale.shape[-1]
    gw = gmlp_ln_g.shape[-1]
    kvw = N_KV_HEADS * HEAD_DIM
    n_q = N_HEADS * HEAD_DIM
    n_kv = 6 * kvw
    n_auv = pw + 2 * gw
    n_gate = 3 * N_HEADS
    alpha = (2 * depth) ** 0.25
    np_rows, ns_rows = batch * seq, nb * tq
    assert pw == gw and seq % KEY_CHUNK == 0 and seq % Q_BLOCK == 0 and tq <= CHUNK and tq % 8 == 0
    assert keep == WINDOW and keep % page == 0 and page % CMP_STRIDE == 0 and np_rows % tq == 0
    nc_p = (seq - CMP_LEN) // CMP_STRIDE + 1
    nc_s = (past + tq - CMP_LEN) // CMP_STRIDE + 1
    assert nc_p < seq // CMP_STRIDE and nc_s < past // CMP_STRIDE
    nsb_p = -(-seq // SEL_LEN)
    nsb_s = -(-(past + tq) // SEL_LEN)
    assert nsb_p <= LANES and nsb_s <= LANES

    w_main = w_in[:, :, :n_auv + n_q + n_kv].astype(BF16)
    w_g = jnp.pad(w_in[:, :, n_auv + n_q + n_kv:n_auv + n_q + n_kv + n_gate], ((0, 0), (0, 0), (0, LANES - n_gate))).astype(BF16)
    w_merge = w_in[:, :, n_auv + n_q + n_kv + n_gate:].astype(BF16)
    w_bp, w_bg, w_bn, w_o = (w.astype(BF16) for w in (w_branch_pool, w_branch_gmlp, w_branch_nsa, w_out))
    wg_b, wu_b, wd_b = w_gate.astype(BF16), w_up.astype(BF16), w_down.astype(BF16)
    w_router_t = w_router.T.astype(F32)
    b_router_c = b_router.astype(F32)[:, None]
    grp_w = gw // GMLP_GROUPS

    pos_all = jnp.concatenate([jnp.tile(jnp.arange(seq), batch), jnp.tile(past + jnp.arange(tq), nb)])
    cos_t, sm_t, sp_t = _rope_tables(pos_all)
    ovl_p = _overlap(nc_p, nsb_p, seq // CMP_STRIDE)
    ovl_s = _overlap(nc_s, nsb_s, past // CMP_STRIDE)
    n_keys_s = (npages + 1) * page
    expand_s = jnp.asarray(np.arange(LANES)[:, None] == (np.arange(n_keys_s)[None, :] // SEL_LEN), BF16)
    expand_p = np.arange(LANES)[:, None] == (np.arange(seq)[None, :] // SEL_LEN)
    expand_p = jnp.asarray(expand_p.reshape(LANES, seq // KEY_CHUNK, KEY_CHUNK).transpose(1, 0, 2), BF16)

    x_all = jnp.concatenate([x_prompt.reshape(np_rows, d), x_sample.reshape(ns_rows, d)], axis=0)
    rows_last = (0, 1, 3, 4, 5, 2)
    cache_cmp_t = cache_cmp_kv.transpose(rows_last)
    cache_sel_t = cache_sel_kv.transpose(rows_last)
    state_win_t = state_win_kv.transpose(rows_last)
    nck = seq // KEY_CHUNK

    st_p = [[] for _ in range(4)]
    st_s = [[] for _ in range(5)]
    for l in range(depth):
        z_auv, z_q, z_kv, z_g = _proj(x_all, w_main[l], w_g[l], cos_t, sm_t, sp_t, n_auv, n_q, n_kv)

        pool_bd = _block_diag([pool_w[l, g] for g in range(len(POOL_WINDOWS))]).astype(BF16)
        ps, lg, lb = pool_scale[l][None, :], gmlp_ln_g[l][None, :], gmlp_ln_b[l][None, :]
        cl = min(CHUNK, seq)
        bs_rows = jnp.repeat(gmlp_bs[l][:, :cl].T, grp_w, axis=1)
        op_p, og_p = _mix_prompt(z_auv, batch, seq, pool_bd, ps, lg, lb, gmlp_ws[l], bs_rows)
        a_t = z_auv[np_rows:].reshape(nb, tq, n_auv).transpose(1, 0, 2)
        st_t = state_pool[l].transpose(1, 0, 2)
        wrow = jnp.repeat(gmlp_ws[l][:, :tq, :tq].transpose(1, 2, 0), grp_w, axis=2).reshape(tq * tq, gw)
        bsrow = jnp.repeat(gmlp_bs[l][:, :tq].T, grp_w, axis=1)
        op_s, og_s, vn_s = _mix_sample(a_t, st_t, float(past), pool_bd, ps, lg, lb, wrow, bsrow)
        to_rows = lambda y: y.transpose(1, 0, 2).reshape(ns_rows, -1)
        o_pool = jnp.concatenate([op_p, to_rows(op_s)], axis=0)
        o_gmlp = jnp.concatenate([og_p, to_rows(og_s)], axis=0)

        w1_bd, pe_rows = _cmp_weights(cmp_w1[l], cmp_pe[l])
        b1 = cmp_b1[l][:, None, :]
        w2kt = cmp_w2[l, 0].T.astype(BF16)
        w2v = cmp_w2[l, 1].astype(BF16)
        h1_p = _cmp1(z_kv, lambda tr: 0, np_rows, w1_bd, pe_rows)
        kct_p, vc_p = _cmp2_prompt(h1_p.reshape(batch, seq // CMP_STRIDE, -1), b1, w2kt, w2v)
        h1_pool = _cmp1_pages(cache_cmp_t, l, w1_bd, pe_rows)
        kct_s, vc_s = _cmp2_sample(page_table, h1_pool.reshape(n_pool, page // CMP_STRIDE, -1), b1, w2kt, w2v)

        zkv_p = z_kv[:np_rows].reshape(batch, seq, 3, 2, N_KV_HEADS, HEAD_DIM)
        o_p = _nsa_prompt(z_q, z_g, kct_p, vc_p,
                          _chunked_kt(zkv_p[:, :, 1, 0], nck), _chunked_v(zkv_p[:, :, 1, 1], nck),
                          _chunked_kt(zkv_p[:, :, 2, 0], nck), _chunked_v(zkv_p[:, :, 2, 1], nck),
                          ovl_p, expand_p, batch, seq)
        o_s = _nsa_sample(page_table, z_q, z_g, z_kv, np_rows, kct_s, vc_s, ovl_s, expand_s, cache_sel_t, state_win_t,
                          l, tq, past, nsb_s)
        o_nsa = jnp.concatenate([o_p, o_s.astype(BF16)], axis=0)

        x1, comb_t = _merge(x_all, o_pool, o_gmlp, o_nsa, w_merge[l], w_bp[l], w_bg[l], w_bn[l], w_o[l],
                            ln1_g[l][None, :], ln1_b[l][None, :], w_router_t, b_router_c, alpha)
        x_all = _moe(x1, comb_t.T, wg_b, wu_b, wd_b, l, ln2_g[l][None, :], ln2_b[l][None, :], alpha)

        kv_shape = (2, N_KV_HEADS, HEAD_DIM)
        zkv_s = z_kv[np_rows:].reshape(nb, tq, 3, *kv_shape)
        a_p = z_auv[:np_rows, :pw].reshape(batch, seq, pw)
        a_s = z_auv[np_rows:, :pw].reshape(nb, tq, pw)
        zkv_p = zkv_p.reshape(batch, seq, 3, *kv_shape)
        st_p[0].append(zkv_p[:, :, 0])
        st_p[1].append(zkv_p[:, :, 1])
        st_p[2].append(zkv_p[:, seq - min(WINDOW, seq):, 2])
        st_p[3].append(a_p[:, seq - POOL_KEEP:])
        st_s[0].append(zkv_s[:, :, 0])
        st_s[1].append(zkv_s[:, :, 1])
        st_s[2].append(jnp.concatenate([state_win_kv[l], zkv_s[:, :, 2]], axis=1)[:, tq:])
        st_s[3].append(jnp.concatenate([state_pool[l], a_s], axis=1)[:, tq:])
        st_s[4].append(vn_s.transpose(1, 0, 2))

    y_prompt = x_all[:np_rows].reshape(batch, seq, d)
    y_sample = x_all[np_rows:].reshape(nb, tq, d)
    return (y_prompt, y_sample,
            jnp.stack(st_p[0]), jnp.stack(st_p[1]), jnp.stack(st_p[2]), jnp.stack(st_p[3]),
            jnp.stack(st_s[0]), jnp.stack(st_s[1]), jnp.stack(st_s[2]), jnp.stack(st_s[3]), jnp.stack(st_s[4]))
```

```python
import functools
import math

import numpy as np
import jax
import jax.numpy as jnp
from jax import lax
from jax.experimental import pallas as pl
from jax.experimental.pallas import tpu as pltpu

F32 = jnp.float32
BF16 = jnp.bfloat16

POOL_WINDOWS = (2, 4, 8, 16)
POOL_KEEP = max(POOL_WINDOWS) - 1
GMLP_GROUPS = 4
CHUNK = 128
N_HEADS = 8
HEAD_DIM = 64
N_KV_HEADS = 2
GQA = N_HEADS // N_KV_HEADS
ROT_DIM = HEAD_DIM // 4
ROPE_THETA = 500000.0
CMP_LEN = 32
CMP_STRIDE = 16
CMP_HIDDEN = 2 * HEAD_DIM
SEL_LEN = 64
SEL_TOPN = 16
WINDOW = 512
Q_BLOCK = 256
N_EXPERTS = 16
EXPERTS_PER_GROUP = 4
LN_EPS = 1e-5
NEG_INF = -1e30
FORCE = 1e9

LANES = 128
KEY_CHUNK = 512
SEL_ROWS = LANES - HEAD_DIM
VMEM_LIMIT = 56 * 1024 * 1024


def _cparams(*sem):
    return pltpu.CompilerParams(dimension_semantics=sem, vmem_limit_bytes=VMEM_LIMIT)


def _tile(n, cap):
    t = cap
    while n % t:
        t //= 2
    return t


def _dot(a, b):
    return jnp.dot(a, b, preferred_element_type=F32)


def _dot_nt(a, b):
    return lax.dot_general(a, b, (((1,), (1,)), ((), ())), preferred_element_type=F32)


def _layer_norm(x, g, b):
    mu = jnp.mean(x, -1, keepdims=True)
    xc = x - mu
    var = jnp.mean(xc * xc, -1, keepdims=True)
    return xc * lax.rsqrt(var + LN_EPS) * g + b


def _masked_softmax(s, mask):
    sm = jnp.where(mask, s, NEG_INF)
    p = jnp.where(mask, jnp.exp(sm - jnp.max(sm, -1, keepdims=True)), 0.0)
    d = jnp.sum(p, -1, keepdims=True)
    return p / jnp.where(d > 0, d, 1.0)


def _chunked_attention(lhs, rhs_fn, v_ref, bias_fn, c_lo, c_hi, s_sc, mx_sc, acc_sc):
    rows, width = mx_sc.shape
    reps = rows // Q_BLOCK

    mx_sc[...] = jnp.full(mx_sc.shape, -jnp.inf, F32)

    @pl.loop(c_lo, c_hi)
    def _(c):
        s = _dot(lhs, rhs_fn(c)) + jnp.concatenate([bias_fn(c)] * reps, axis=0)
        s_sc[c] = s
        mx = mx_sc[...]
        for j in range(s.shape[1] // width):
            mx = jnp.maximum(mx, s[:, j * width:(j + 1) * width])
        mx_sc[...] = mx

    m = jnp.max(mx_sc[...], -1, keepdims=True)
    acc_sc[...] = jnp.zeros(acc_sc.shape, F32)

    @pl.loop(c_lo, c_hi)
    def _(c):
        p = jnp.exp(s_sc[c] - m).astype(BF16)
        acc_sc[...] += _dot(p, v_ref[c])

    acc = acc_sc[...]
    return acc / pltpu.roll(acc, acc.shape[1] // 2, 1)


def _topk_rows(score_t, k, n_valid):
    nq = score_t.shape[1]
    sub_rows = 8
    nv = -(-n_valid // sub_rows)
    tiles = [score_t[sub_rows * v:sub_rows * (v + 1)] for v in range(nv)]
    sub = lax.broadcasted_iota(jnp.int32, (sub_rows, nq), 0)
    ranks = [jnp.zeros((sub_rows, nq), F32) for _ in range(nv)]
    for i in range(n_valid):
        vi, si = divmod(i, sub_rows)
        r = jnp.broadcast_to(score_t[i:i + 1], (sub_rows, nq))
        for v in range(nv):
            if v < vi:
                ahead = r > tiles[v]
            elif v > vi:
                ahead = r >= tiles[v]
            else:
                ahead = (r > tiles[v]) | ((r >= tiles[v]) & (sub > si))
            ranks[v] = ranks[v] + jnp.where(ahead, 1.0, 0.0)
    sel = [jnp.where((ranks[v] < k) & (sub_rows * v + sub < n_valid), 1.0, 0.0) for v in range(nv)]
    pad = score_t.shape[0] - sub_rows * nv
    if pad:
        sel.append(jnp.zeros((pad, nq), F32))
    return jnp.concatenate(sel, axis=0)


def _importance(p_cmp, rows, ovl):
    psum = p_cmp[0:rows]
    for g in range(1, GQA):
        psum = psum + p_cmp[g * rows:(g + 1) * rows]
    hi = psum.astype(BF16)
    lo = (psum - hi.astype(F32)).astype(BF16)
    return _dot(hi, ovl) + _dot(lo, ovl)


def _select_blocks(imp, qpos_t, nsb):
    imp_t = imp.T
    blk = lax.broadcasted_iota(jnp.int32, imp_t.shape, 0)
    cur = jnp.right_shift(qpos_t, int(math.log2(SEL_LEN)))
    forced = (blk == 0) | (blk == cur) | (blk == cur - 1)
    score_t = jnp.where(forced, FORCE, jnp.where(blk <= cur, imp_t, -FORCE))
    return _topk_rows(score_t, min(SEL_TOPN, nsb), nsb).T


def _head_gate(gs, col):
    lane = lax.broadcasted_iota(jnp.int32, gs.shape, 1)
    return jnp.sum(jnp.where(lane == col, gs, 0.0), -1, keepdims=True)


def _proj_kernel(x_ref, w_ref, wg_ref, cos_ref, sm_ref, sp_ref, auv_ref, q_ref, kv_ref, g_ref, *, n_auv, n_q, n_kv):
    xb = x_ref[...].astype(BF16)
    z = _dot(xb, w_ref[...])
    auv_ref[...] = z[:, :n_auv]
    cos, sm, sp = cos_ref[...], sm_ref[...], sp_ref[...]
    half = ROT_DIM // 2

    def rope(t):
        return t * cos + pltpu.roll(t, LANES - half, 1) * sm + pltpu.roll(t, half, 1) * sp

    for j in range(n_q // LANES):
        c0 = n_auv + j * LANES
        q_ref[:, j * LANES:(j + 1) * LANES] = rope(z[:, c0:c0 + LANES]) * (HEAD_DIM ** -0.5)
    for j in range(n_kv // LANES):
        c0 = n_auv + n_q + j * LANES
        t = z[:, c0:c0 + LANES]
        kv_ref[:, j * LANES:(j + 1) * LANES] = rope(t) if j % 2 == 0 else t
    g_ref[...] = _dot(xb, wg_ref[...])


def _proj(x, w_main, w_gate, cos, sm, sp, n_auv, n_q, n_kv):
    n, d = x.shape
    tm = _tile(n, 512)
    row = lambda i: (i, 0)
    fixed = lambda i: (0, 0)
    return pl.pallas_call(
        functools.partial(_proj_kernel, n_auv=n_auv, n_q=n_q, n_kv=n_kv),
        grid=(n // tm,),
        in_specs=[pl.BlockSpec((tm, d), row),
                  pl.BlockSpec(w_main.shape, fixed),
                  pl.BlockSpec(w_gate.shape, fixed),
                  pl.BlockSpec((tm, LANES), row), pl.BlockSpec((tm, LANES), row), pl.BlockSpec((tm, LANES), row)],
        out_specs=[pl.BlockSpec((tm, n_auv), row), pl.BlockSpec((tm, n_q), row),
                   pl.BlockSpec((tm, n_kv), row), pl.BlockSpec((tm, LANES), row)],
        out_shape=[jax.ShapeDtypeStruct((n, n_auv), F32), jax.ShapeDtypeStruct((n, n_q), F32),
                   jax.ShapeDtypeStruct((n, n_kv), F32), jax.ShapeDtypeStruct((n, LANES), F32)],
        compiler_params=_cparams("parallel"),
        name="in_proj",
    )(x, w_main, w_gate, cos, sm, sp)


def _pool_select(grp, vals):
    out = vals[-1]
    for gi in range(len(vals) - 2, -1, -1):
        out = jnp.where(grp == gi, vals[gi], out)
    return out


def _mix_prompt_kernel(auv_ref, halo_ref, pw_ref, ps_ref, lng_ref, lnb_ref, ws_ref, bsr_ref, op_ref, og_ref, *, tm, pw):
    i = pl.program_id(1)
    halo_rows = halo_ref.shape[0]
    a = auv_ref[:, 0:pw]
    u = auv_ref[:, pw:2 * pw]
    v = auv_ref[:, 2 * pw:3 * pw]
    halo = jnp.where(i == 0, 0.0, halo_ref[...])
    ext = jnp.concatenate([halo, a], axis=0)
    sums, cur, width = [], ext, 1
    for win in POOL_WINDOWS:
        while width < win:
            cur = cur + pltpu.roll(cur, width, 0)
            width *= 2
        sums.append(cur[halo_rows:])
    lane = lax.broadcasted_iota(jnp.int32, (tm, pw), 1)
    grp = lane // (pw // len(POOL_WINDOWS))
    pos = (i * tm + lax.broadcasted_iota(jnp.int32, (tm, pw), 0)).astype(F32)
    total = _pool_select(grp, sums)
    win = _pool_select(grp, [jnp.full((tm, pw), float(w), F32) for w in POOL_WINDOWS])
    d = total / jnp.minimum(win, pos + 1.0) - a
    op_ref[...] = (_dot(d.astype(BF16), pw_ref[...]) * ps_ref[...]).astype(op_ref.dtype)

    vn = _layer_norm(v, lng_ref[...], lnb_ref[...]).astype(BF16)
    r = lax.broadcasted_iota(jnp.int32, (CHUNK, CHUNK), 0)
    c = lax.broadcasted_iota(jnp.int32, (CHUNK, CHUNK), 1)
    wms = [jnp.where(c <= r, ws_ref[g], 0.0).astype(BF16) for g in range(GMLP_GROUPS)]
    grp_c = lax.broadcasted_iota(jnp.int32, (CHUNK, pw), 1) // (pw // GMLP_GROUPS)
    for j in range(tm // CHUNK):
        rows = slice(j * CHUNK, (j + 1) * CHUNK)
        vc = vn[rows]
        mix = _pool_select(grp_c, [_dot(wms[g], vc) for g in range(GMLP_GROUPS)])
        og_ref[rows, :] = (u[rows] * (mix + bsr_ref[...])).astype(og_ref.dtype)


def _mix_prompt(z_auv, batch, seq, pool_bd, pool_scale, ln_g, ln_b, ws, bs_rows):
    pw = pool_scale.shape[-1]
    tm = _tile(seq, 512)
    nt = seq // tm
    halo = 2 * 8
    hb = tm // halo
    fixed2 = lambda b, i: (0, 0)
    return pl.pallas_call(
        functools.partial(_mix_prompt_kernel, tm=tm, pw=pw),
        grid=(batch, nt),
        in_specs=[pl.BlockSpec((tm, 3 * pw), lambda b, i: (b * nt + i, 0)),
                  pl.BlockSpec((halo, pw), lambda b, i: (jnp.maximum((b * nt + i) * hb - 1, 0), 0)),
                  pl.BlockSpec((pw, pw), fixed2), pl.BlockSpec((1, pw), fixed2),
                  pl.BlockSpec((1, pw), fixed2), pl.BlockSpec((1, pw), fixed2),
                  pl.BlockSpec(ws.shape, lambda b, i: (0, 0, 0)),
                  pl.BlockSpec((CHUNK, pw), fixed2)],
        out_specs=[pl.BlockSpec((tm, pw), lambda b, i: (b * nt + i, 0)),
                   pl.BlockSpec((tm, pw), lambda b, i: (b * nt + i, 0))],
        out_shape=[jax.ShapeDtypeStruct((batch * seq, pw), BF16), jax.ShapeDtypeStruct((batch * seq, pw), BF16)],
        compiler_params=_cparams("parallel", "parallel"),
        name="mix_prompt",
    )(z_auv, z_auv, pool_bd, pool_scale, ln_g, ln_b, ws, bs_rows)


def _mix_sample_kernel(a_ref, st_ref, pw_ref, ps_ref, lng_ref, lnb_ref, wrow_ref, bsr_ref, op_ref, og_ref, vn_ref,
                       *, tq, keep, pos0, pw):
    nb = a_ref.shape[1]
    ext = [st_ref[i] for i in range(keep)] + [a_ref[t][:, 0:pw] for t in range(tq)]
    grp = lax.broadcasted_iota(jnp.int32, (nb, pw), 1) // (pw // len(POOL_WINDOWS))
    for t in range(tq):
        e = keep + t
        acc, k, sums = ext[e], 1, []
        for win in POOL_WINDOWS:
            while k < win:
                if e - k >= 0:
                    acc = acc + ext[e - k]
                k += 1
            sums.append(acc)
        total = _pool_select(grp, sums)
        cnt = _pool_select(grp, [jnp.full((nb, pw), min(float(w), pos0 + t + 1.0), F32) for w in POOL_WINDOWS])
        d = total / cnt - ext[e]
        op_ref[t] = (_dot(d.astype(BF16), pw_ref[...]) * ps_ref[...]).astype(op_ref.dtype)
    vns = []
    for t in range(tq):
        vn = _layer_norm(a_ref[t][:, 2 * pw:3 * pw], lng_ref[...], lnb_ref[...])
        vn_ref[t] = vn
        vns.append(vn)
    for t in range(tq):
        mix = bsr_ref[t:t + 1, :]
        for s in range(t + 1):
            mix = mix + wrow_ref[t * tq + s:t * tq + s + 1, :] * vns[s]
        og_ref[t] = (a_ref[t][:, pw:2 * pw] * mix).astype(og_ref.dtype)


def _mix_sample(a_t, st_t, pos0, pool_bd, pool_scale, ln_g, ln_b, wrow, bsrow):
    tq, nb, w3 = a_t.shape
    keep = st_t.shape[0]
    pw = w3 // 3
    full = lambda shape: pl.BlockSpec(shape, lambda i: (0,) * len(shape))
    return pl.pallas_call(
        functools.partial(_mix_sample_kernel, tq=tq, keep=keep, pos0=pos0, pw=pw),
        grid=(1,),
        in_specs=[full(a_t.shape), full(st_t.shape), full(pool_bd.shape), full(pool_scale.shape),
                  full(ln_g.shape), full(ln_b.shape), full(wrow.shape), full(bsrow.shape)],
        out_specs=[full((tq, nb, pw)), full((tq, nb, pw)), full((tq, nb, pw))],
        out_shape=[jax.ShapeDtypeStruct((tq, nb, pw), BF16), jax.ShapeDtypeStruct((tq, nb, pw), BF16),
                   jax.ShapeDtypeStruct((tq, nb, pw), F32)],
        compiler_params=_cparams("arbitrary"),
        name="mix_sample",
    )(a_t, st_t, pool_bd, pool_scale, ln_g, ln_b, wrow, bsrow)


def _cmp1_kernel(x_ref, w_ref, pe_ref, h_ref, *, g):
    hw = N_KV_HEADS * CMP_HIDDEN
    xcat = jnp.concatenate([x_ref[pl.ds(r, g, stride=CMP_STRIDE), :] for r in range(CMP_STRIDE)], axis=1)
    for ab in range(2):
        lhs = (xcat + pe_ref[ab]).astype(BF16)
        h_ref[:, ab * hw:(ab + 1) * hw] = _dot(lhs, w_ref[ab])


def _cmp1(x2d, row_block0, n_rows, w1_bd, pe_rows):
    tr = _tile(n_rows, 4096)
    g = tr // CMP_STRIDE
    kvw = N_KV_HEADS * HEAD_DIM
    hcols = 2 * N_KV_HEADS * CMP_HIDDEN
    return pl.pallas_call(
        functools.partial(_cmp1_kernel, g=g),
        grid=(n_rows // tr, 2),
        in_specs=[pl.BlockSpec((tr, kvw), lambda i, kv: (row_block0(tr) + i, kv)),
                  pl.BlockSpec((None,) + w1_bd.shape[1:], lambda i, kv: (kv, 0, 0, 0)),
                  pl.BlockSpec((None,) + pe_rows.shape[1:], lambda i, kv: (kv, 0, 0, 0))],
        out_specs=pl.BlockSpec((g, hcols), lambda i, kv: (i, kv)),
        out_shape=jax.ShapeDtypeStruct((n_rows // CMP_STRIDE, 2 * hcols), F32),
        compiler_params=_cparams("parallel", "parallel"),
        name="cmp_stage1",
    )(x2d, w1_bd, pe_rows)


def _cmp1_pages_kernel(xt_ref, w_ref, pe_ref, h_ref, x_sc, *, g):
    n_pages, n_h, hd, page = xt_ref.shape
    for p in range(n_pages):
        x_sc[p * page:(p + 1) * page, :] = xt_ref[p].reshape(n_h * hd, page).T
    _cmp1_kernel(x_sc, w_ref, pe_ref, h_ref, g=g)


def _cmp1_pages(cache_t, layer, w1_bd, pe_rows):
    _, n_pool, _, n_h, hd, page = cache_t.shape
    pp = _tile(n_pool, 4096 // page)
    g = pp * page // CMP_STRIDE
    hcols = 2 * N_KV_HEADS * CMP_HIDDEN
    return pl.pallas_call(
        functools.partial(_cmp1_pages_kernel, g=g),
        grid=(n_pool // pp, 2),
        in_specs=[pl.BlockSpec((None, pp, None, n_h, hd, page), lambda i, kv: (layer, i, kv, 0, 0, 0)),
                  pl.BlockSpec((None,) + w1_bd.shape[1:], lambda i, kv: (kv, 0, 0, 0)),
                  pl.BlockSpec((None,) + pe_rows.shape[1:], lambda i, kv: (kv, 0, 0, 0))],
        out_specs=pl.BlockSpec((g, hcols), lambda i, kv: (i, kv)),
        out_shape=jax.ShapeDtypeStruct((n_pool * page // CMP_STRIDE, 2 * hcols), F32),
        scratch_shapes=[pltpu.VMEM((pp * page, n_h * hd), F32)],
        compiler_params=_cparams("parallel", "parallel"),
        name="cmp_stage1_pages",
    )(cache_t, w1_bd, pe_rows)


def _gelu_tanh(x):
    return 0.5 * x * (1.0 + jnp.tanh(math.sqrt(2.0 / math.pi) * (x + 0.044715 * (x * x * x))))


def _cmp2_body(h_refs, b1_ref, w2kt_ref, w2v_ref, kct_ref, vc_ref):
    hall = jnp.concatenate([r[...] for r in h_refs], axis=0) if len(h_refs) > 1 else h_refs[0][...]
    g = hall.shape[0]
    hw = N_KV_HEADS * CMP_HIDDEN
    for kv in range(2):
        for h in range(N_KV_HEADS):
            ca = (kv * 2) * hw + h * CMP_HIDDEN
            cb = (kv * 2 + 1) * hw + h * CMP_HIDDEN
            hid = hall[:, ca:ca + CMP_HIDDEN] + pltpu.roll(hall[:, cb:cb + CMP_HIDDEN], g - 1, 0) + b1_ref[kv]
            hid = _gelu_tanh(hid).astype(BF16)
            if kv == 0:
                kct_ref[h] = _dot_nt(w2kt_ref[...], hid).astype(kct_ref.dtype)
            else:
                vc_ref[h] = _dot(hid, w2v_ref[...]).astype(vc_ref.dtype)


def _cmp2_prompt_kernel(h_ref, b1_ref, w2kt_ref, w2v_ref, kct_ref, vc_ref):
    _cmp2_body([h_ref], b1_ref, w2kt_ref, w2v_ref, kct_ref, vc_ref)


def _cmp2_sample_kernel(pt_ref, *refs, npages):
    _cmp2_body(list(refs[:npages]), *refs[npages:])


def _cmp2_out(n, g):
    specs = [pl.BlockSpec((None, N_KV_HEADS, HEAD_DIM, g), lambda b, *_: (b, 0, 0, 0)),
             pl.BlockSpec((None, N_KV_HEADS, g, HEAD_DIM), lambda b, *_: (b, 0, 0, 0))]
    shapes = [jax.ShapeDtypeStruct((n, N_KV_HEADS, HEAD_DIM, g), BF16),
              jax.ShapeDtypeStruct((n, N_KV_HEADS, g, HEAD_DIM), BF16)]
    return specs, shapes


def _cmp2_prompt(h1, b1, w2kt, w2v):
    n, g, hc = h1.shape
    out_specs, out_shape = _cmp2_out(n, g)
    return pl.pallas_call(
        _cmp2_prompt_kernel,
        grid=(n,),
        in_specs=[pl.BlockSpec((None, g, hc), lambda b: (b, 0, 0)),
                  pl.BlockSpec(b1.shape, lambda b: (0, 0, 0)),
                  pl.BlockSpec(w2kt.shape, lambda b: (0, 0)),
                  pl.BlockSpec(w2v.shape, lambda b: (0, 0))],
        out_specs=out_specs, out_shape=out_shape,
        compiler_params=_cparams("parallel"),
        name="cmp_stage2_prompt",
    )(h1, b1, w2kt, w2v)


def _cmp2_sample(page_table, h1_pool, b1, w2kt, w2v):
    n, npages = page_table.shape
    _, gp, hc = h1_pool.shape
    out_specs, out_shape = _cmp2_out(n, npages * gp)
    page_specs = [pl.BlockSpec((None, gp, hc), functools.partial(lambda b, pt, j: (pt[b, j], 0, 0), j=j))
                  for j in range(npages)]
    return pl.pallas_call(
        functools.partial(_cmp2_sample_kernel, npages=npages),
        grid_spec=pltpu.PrefetchScalarGridSpec(
            num_scalar_prefetch=1, grid=(n,),
            in_specs=page_specs + [pl.BlockSpec(b1.shape, lambda b, pt: (0, 0, 0)),
                                   pl.BlockSpec(w2kt.shape, lambda b, pt: (0, 0)),
                                   pl.BlockSpec(w2v.shape, lambda b, pt: (0, 0))],
            out_specs=out_specs),
        out_shape=out_shape,
        compiler_params=_cparams("parallel"),
        name="cmp_stage2_sample",
    )(page_table, *([h1_pool] * npages), b1, w2kt, w2v)


def _nsa_prompt_kernel(q_ref, g_ref, kct_ref, vc_ref, kts_ref, vs_ref, ktw_ref, vw_ref, ovl_ref, exp_ref, o_ref,
                       s_sc, mx_sc, acc_sc, *, nsb):
    h = pl.program_id(1)
    qb = pl.program_id(2)
    tq = Q_BLOCK
    rows = GQA * tq
    t0 = qb * tq
    qf = q_ref[...]
    q = jnp.concatenate([qf[:, g * HEAD_DIM:(g + 1) * HEAD_DIM] for g in range(GQA)], axis=0).astype(BF16)
    qpos_r = t0 + (lax.broadcasted_iota(jnp.int32, (rows, 1), 0) & (tq - 1))

    ncp = kct_ref.shape[-1]
    s = _dot(q, kct_ref[...])
    cend = lax.broadcasted_iota(jnp.int32, (1, ncp), 1) * CMP_STRIDE + (CMP_LEN - 1)
    p_cmp = _masked_softmax(s, cend <= qpos_r)
    o_cmp = _dot(p_cmp.astype(BF16), vc_ref[...])

    imp = _importance(p_cmp, tq, ovl_ref[...])
    qpos_t = t0 + lax.broadcasted_iota(jnp.int32, (imp.shape[1], imp.shape[0]), 1)
    selm = _select_blocks(imp, qpos_t, nsb)
    qpos_c = t0 + lax.broadcasted_iota(jnp.int32, (tq, 1), 0)
    kcol = lax.broadcasted_iota(jnp.int32, (1, KEY_CHUNK), 1)

    hidden = jnp.concatenate([1.0 - selm[:, :SEL_ROWS]] * GQA, axis=0).astype(BF16)
    q_sel = jnp.concatenate([q, hidden], axis=1)

    def sel_rhs(c):
        return jnp.concatenate([kts_ref[c], exp_ref[c]], axis=0)

    def sel_bias(c):
        return jnp.where(c * KEY_CHUNK + kcol <= qpos_c, 0.0, NEG_INF)

    def win_bias(c):
        kpos = c * KEY_CHUNK + kcol
        return jnp.where((kpos <= qpos_c) & (kpos > qpos_c - WINDOW), 0.0, NEG_INF)

    c_hi = (t0 + tq - 1) // KEY_CHUNK + 1
    c_lo = jnp.maximum(t0 - WINDOW + 1, 0) // KEY_CHUNK
    o_sel = _chunked_attention(q_sel, sel_rhs, vs_ref, sel_bias, 0, c_hi, s_sc, mx_sc, acc_sc)
    o_win = _chunked_attention(q, lambda c: ktw_ref[c], vw_ref, win_bias, c_lo, c_hi, s_sc, mx_sc, acc_sc)

    gs = jax.nn.sigmoid(g_ref[...])
    outs = []
    for g in range(GQA):
        r = slice(g * tq, (g + 1) * tq)
        col = (h * GQA + g) * 3
        outs.append(_head_gate(gs, col) * o_cmp[r] + _head_gate(gs, col + 1) * o_sel[r, :HEAD_DIM]
                    + _head_gate(gs, col + 2) * o_win[r, :HEAD_DIM])
    o_ref[...] = jnp.concatenate(outs, axis=1).astype(o_ref.dtype)


def _nsa_prompt(z_q, z_g, kct, vc, kts, vs, ktw, vw, ovl, expand, batch, seq):
    nqb = seq // Q_BLOCK
    hq = GQA * HEAD_DIM
    nsb = -(-seq // SEL_LEN)
    per_head = lambda shape: pl.BlockSpec((None, None) + shape, lambda b, h, i: (b, h) + (0,) * len(shape))
    return pl.pallas_call(
        functools.partial(_nsa_prompt_kernel, nsb=nsb),
        grid=(batch, N_KV_HEADS, nqb),
        in_specs=[pl.BlockSpec((Q_BLOCK, hq), lambda b, h, i: (b * nqb + i, h)),
                  pl.BlockSpec((Q_BLOCK, LANES), lambda b, h, i: (b * nqb + i, 0)),
                  per_head(kct.shape[2:]), per_head(vc.shape[2:]),
                  per_head(kts.shape[2:]), per_head(vs.shape[2:]),
                  per_head(ktw.shape[2:]), per_head(vw.shape[2:]),
                  pl.BlockSpec(ovl.shape, lambda b, h, i: (0, 0)),
                  pl.BlockSpec(expand.shape, lambda b, h, i: (0, 0, 0))],
        out_specs=pl.BlockSpec((Q_BLOCK, hq), lambda b, h, i: (b * nqb + i, h)),
        out_shape=jax.ShapeDtypeStruct((batch * seq, N_KV_HEADS * hq), BF16),
        scratch_shapes=[pltpu.VMEM((seq // KEY_CHUNK, GQA * Q_BLOCK, KEY_CHUNK), F32),
                        pltpu.VMEM((GQA * Q_BLOCK, LANES), F32), pltpu.VMEM((GQA * Q_BLOCK, vs.shape[-1]), F32)],
        compiler_params=_cparams("parallel", "parallel", "parallel"),
        name="nsa_prompt",
    )(z_q, z_g, kct, vc, kts, vs, ktw, vw, ovl, expand)


def _nsa_sample_kernel(pt_ref, q_ref, g_ref, kvn_ref, kct_ref, vc_ref, ovl_ref, exp_ref, win_ref, *rest,
                       npages, tq, pos0, nsb):
    page_refs = rest[:npages]
    o_ref = rest[npages]
    newpg = rest[npages + 1]
    page = newpg.shape[0]
    kvw = N_KV_HEADS * HEAD_DIM
    newpg[...] = jnp.zeros(newpg.shape, newpg.dtype)
    newpg[0:tq, :] = kvn_ref[...]
    rows = GQA * tq
    qpos_r = pos0 + lax.rem(lax.broadcasted_iota(jnp.int32, (rows, 1), 0), tq)
    qf = q_ref[...]
    gs = jax.nn.sigmoid(g_ref[...])
    keep = win_ref.shape[-1]
    heads = range(N_KV_HEADS)
    qs = [jnp.concatenate([qf[:, (h * GQA + g) * HEAD_DIM:(h * GQA + g + 1) * HEAD_DIM] for g in range(GQA)],
                          axis=0).astype(BF16) for h in heads]

    ncp = kct_ref.shape[-1]
    cend = lax.broadcasted_iota(jnp.int32, (1, ncp), 1) * CMP_STRIDE + (CMP_LEN - 1)
    p_cmps = [_masked_softmax(_dot(qs[h], kct_ref[h]), cend <= qpos_r) for h in heads]
    o_cmps = [_dot(p_cmps[h].astype(BF16), vc_ref[h]) for h in heads]

    imps = [_importance(p_cmps[h], tq, ovl_ref[...]) for h in heads]
    imp = jnp.concatenate(imps + [jnp.zeros((LANES - N_KV_HEADS * tq, LANES), F32)], axis=0)
    qpos_t = pos0 + lax.rem(lax.broadcasted_iota(jnp.int32, (LANES, LANES), 1), tq)
    selm_all = _select_blocks(imp, qpos_t, nsb)

    outs = []
    for h in heads:
        q, o_cmp = qs[h], o_cmps[h]
        kc0 = h * HEAD_DIM
        vc0 = kvw + h * HEAD_DIM

        selm = selm_all[h * tq:(h + 1) * tq]
        e = _dot(jnp.concatenate([selm] * GQA, axis=0).astype(BF16), exp_ref[...])
        k_new = newpg[:, 2 * kvw + kc0:2 * kvw + kc0 + HEAD_DIM].astype(BF16)
        v_new = newpg[:, 2 * kvw + vc0:2 * kvw + vc0 + HEAD_DIM].astype(BF16)
        s = jnp.concatenate([_dot(q, page_refs[p][0, h].astype(BF16)) for p in range(npages)] + [_dot_nt(q, k_new)], axis=1)
        kpos = lax.broadcasted_iota(jnp.int32, (1, s.shape[1]), 1)
        p = _masked_softmax(s, (e > 0.5) & (kpos <= qpos_r)).astype(BF16)
        o_sel = _dot(p[:, npages * page:], v_new)
        for j in range(npages):
            o_sel = o_sel + _dot_nt(p[:, j * page:(j + 1) * page], page_refs[j][1, h].astype(BF16))

        k_new = newpg[:, 4 * kvw + kc0:4 * kvw + kc0 + HEAD_DIM].astype(BF16)
        v_new = newpg[:, 4 * kvw + vc0:4 * kvw + vc0 + HEAD_DIM].astype(BF16)
        s = jnp.concatenate([_dot(q, win_ref[0, h].astype(BF16)), _dot_nt(q, k_new)], axis=1)
        kpos = pos0 - keep + lax.broadcasted_iota(jnp.int32, (1, s.shape[1]), 1)
        p = _masked_softmax(s, (kpos <= qpos_r) & (kpos > qpos_r - WINDOW)).astype(BF16)
        o_win = _dot_nt(p[:, :keep], win_ref[1, h].astype(BF16)) + _dot(p[:, keep:], v_new)

        for g in range(GQA):
            r = slice(g * tq, (g + 1) * tq)
            col = (h * GQA + g) * 3
            outs.append(_head_gate(gs, col) * o_cmp[r] + _head_gate(gs, col + 1) * o_sel[r]
                        + _head_gate(gs, col + 2) * o_win[r])
    o_ref[...] = jnp.concatenate(outs, axis=1).astype(o_ref.dtype)


def _nsa_sample(page_table, z_q, z_g, z_kv, row0, kct, vc, ovl, expand, cache_sel, state_win, layer, tq, pos0, nsb):
    n, npages = page_table.shape
    page = cache_sel.shape[-1]
    blk0 = row0 // tq
    hq = z_q.shape[1]
    page_specs = [pl.BlockSpec((None, None) + cache_sel.shape[2:],
                               functools.partial(lambda b, pt, j: (layer, pt[b, j], 0, 0, 0, 0), j=j))
                  for j in range(npages)]
    return pl.pallas_call(
        functools.partial(_nsa_sample_kernel, npages=npages, tq=tq, pos0=pos0, nsb=nsb),
        grid_spec=pltpu.PrefetchScalarGridSpec(
            num_scalar_prefetch=1, grid=(n,),
            in_specs=[pl.BlockSpec((tq, hq), lambda b, pt: (blk0 + b, 0)),
                      pl.BlockSpec((tq, LANES), lambda b, pt: (blk0 + b, 0)),
                      pl.BlockSpec((tq, z_kv.shape[1]), lambda b, pt: (blk0 + b, 0)),
                      pl.BlockSpec((None,) + kct.shape[1:], lambda b, pt: (b, 0, 0, 0)),
                      pl.BlockSpec((None,) + vc.shape[1:], lambda b, pt: (b, 0, 0, 0)),
                      pl.BlockSpec(ovl.shape, lambda b, pt: (0, 0)),
                      pl.BlockSpec(expand.shape, lambda b, pt: (0, 0)),
                      pl.BlockSpec((None, None) + state_win.shape[2:], lambda b, pt: (layer, b, 0, 0, 0, 0))] + page_specs,
            out_specs=pl.BlockSpec((tq, hq), lambda b, pt: (b, 0)),
            scratch_shapes=[pltpu.VMEM((page, z_kv.shape[1]), F32)]),
        out_shape=jax.ShapeDtypeStruct((n * tq, hq), F32),
        compiler_params=_cparams("parallel"),
        name="nsa_sample",
    )(page_table, z_q, z_g, z_kv, kct, vc, ovl, expand, state_win, *([cache_sel] * npages))


def _route(x1, wrt, br, ct_ref):
    logits = lax.dot_general(wrt, x1, (((1,), (1,)), ((), ())), precision=lax.Precision.HIGHEST,
                             preferred_element_type=F32)
    aff = jax.nn.sigmoid(logits)
    score = aff + br
    sc = [score[e:e + 1, :] for e in range(N_EXPERTS)]
    af = [aff[e:e + 1, :] for e in range(N_EXPERTS)]
    n_grp = N_EXPERTS // EXPERTS_PER_GROUP
    gsum = []
    for k in range(n_grp):
        m = sc[k * EXPERTS_PER_GROUP:(k + 1) * EXPERTS_PER_GROUP]
        best = None
        for i in range(EXPERTS_PER_GROUP):
            for j in range(i + 1, EXPERTS_PER_GROUP):
                pair = m[i] + m[j]
                best = pair if best is None else jnp.maximum(best, pair)
        gsum.append(best)
    is_best, taken = [], None
    for k in range(n_grp):
        ok = None
        for j in range(k + 1, n_grp):
            c = gsum[k] >= gsum[j]
            ok = c if ok is None else ok & c
        if ok is None:
            ok = jnp.ones_like(gsum[k], dtype=jnp.bool_)
        if taken is not None:
            ok = ok & jnp.logical_not(taken)
        is_best.append(ok)
        taken = ok if taken is None else taken | ok
    sel = []
    for e in range(N_EXPERTS):
        k = e // EXPERTS_PER_GROUP
        rank = jnp.zeros_like(sc[e])
        for e2 in range(k * EXPERTS_PER_GROUP, (k + 1) * EXPERTS_PER_GROUP):
            if e2 == e:
                continue
            ahead = (sc[e2] >= sc[e]) if e2 < e else (sc[e2] > sc[e])
            rank = rank + ahead.astype(F32)
        sel.append(is_best[k] & (rank < 2.0))
    denom = jnp.zeros_like(sc[0])
    for e in range(N_EXPERTS):
        denom = denom + jnp.where(sel[e], af[e], 0.0)
    for e in range(N_EXPERTS):
        ct_ref[e:e + 1, :] = jnp.where(sel[e], af[e] / denom, 0.0)


def _merge_kernel(x_ref, op_ref, og_ref, on_ref, wm_ref, wp_ref, wgm_ref, wn_ref, wo_ref, lg_ref, lb_ref,
                  wrt_ref, br_ref, x1_ref, ct_ref, *, alpha):
    x = x_ref[...]
    xb = x.astype(BF16)
    d = x.shape[1]
    merged = None
    for b, (o_r, w_r) in enumerate(((op_ref, wp_ref), (og_ref, wgm_ref), (on_ref, wn_ref))):
        gate = jax.nn.sigmoid(_dot(xb, wm_ref[:, b * d:(b + 1) * d]))
        term = gate * _dot(o_r[...], w_r[...])
        merged = term if merged is None else merged + term
    out = _dot(merged.astype(BF16), wo_ref[...])
    x1 = _layer_norm(alpha * x + out, lg_ref[...], lb_ref[...])
    x1_ref[...] = x1
    _route(x1, wrt_ref[...], br_ref[...], ct_ref)


def _merge(x, o_pool, o_gmlp, o_nsa, w_merge, w_bp, w_bg, w_bn, w_out, ln_g, ln_b, w_router_t, b_router, alpha):
    n, d = x.shape
    tm = _tile(n, 512)
    row = lambda i: (i, 0)
    fixed = lambda a: pl.BlockSpec(a.shape, lambda i: (0, 0))
    return pl.pallas_call(
        functools.partial(_merge_kernel, alpha=alpha),
        grid=(n // tm,),
        in_specs=[pl.BlockSpec((tm, d), row), pl.BlockSpec((tm, o_pool.shape[1]), row),
                  pl.BlockSpec((tm, o_gmlp.shape[1]), row), pl.BlockSpec((tm, o_nsa.shape[1]), row),
                  fixed(w_merge), fixed(w_bp), fixed(w_bg), fixed(w_bn), fixed(w_out), fixed(ln_g), fixed(ln_b),
                  fixed(w_router_t), fixed(b_router)],
        out_specs=[pl.BlockSpec((tm, d), row), pl.BlockSpec((N_EXPERTS, tm), lambda i: (0, i))],
        out_shape=[jax.ShapeDtypeStruct((n, d), F32), jax.ShapeDtypeStruct((N_EXPERTS, n), F32)],
        compiler_params=_cparams("parallel"),
        name="merge_ln_route",
    )(x, o_pool, o_gmlp, o_nsa, w_merge, w_bp, w_bg, w_bn, w_out, ln_g, ln_b, w_router_t, b_router)


def _moe_kernel(x_ref, c_ref, wg_ref, wu_ref, wd_ref, lg_ref, lb_ref, y_ref, xb_s, acc_s, *, alpha):
    e = pl.program_id(1)

    @pl.when(e == 0)
    def _():
        xb_s[...] = x_ref[...].astype(BF16)
        acc_s[...] = jnp.zeros(acc_s.shape, acc_s.dtype)

    xb = xb_s[...]
    hg = _dot(xb, wg_ref[...])
    hu = _dot(xb, wu_ref[...])
    comb = c_ref[...]
    lane = lax.broadcasted_iota(jnp.int32, comb.shape, 1)
    ce = jnp.sum(jnp.where(lane == e, comb, 0.0), -1, keepdims=True)
    act = (hg * jax.nn.sigmoid(hg)) * hu * ce
    acc_s[...] += _dot(act.astype(BF16), wd_ref[...])

    @pl.when(e == pl.num_programs(1) - 1)
    def _():
        y_ref[...] = _layer_norm(alpha * x_ref[...] + acc_s[...], lg_ref[...], lb_ref[...])


def _moe(x, comb, w_gate, w_up, w_down, layer, ln_g, ln_b, alpha):
    n, d = x.shape
    tm = _tile(n, 1024)
    ne, _, de = w_gate.shape[1:]
    return pl.pallas_call(
        functools.partial(_moe_kernel, alpha=alpha),
        grid=(n // tm, ne),
        in_specs=[pl.BlockSpec((tm, d), lambda i, e: (i, 0)),
                  pl.BlockSpec((tm, ne), lambda i, e: (i, 0)),
                  pl.BlockSpec((None, None, d, de), lambda i, e: (layer, e, 0, 0)),
                  pl.BlockSpec((None, None, d, de), lambda i, e: (layer, e, 0, 0)),
                  pl.BlockSpec((None, None, de, d), lambda i, e: (layer, e, 0, 0)),
                  pl.BlockSpec(ln_g.shape, lambda i, e: (0, 0)),
                  pl.BlockSpec(ln_b.shape, lambda i, e: (0, 0))],
        out_specs=pl.BlockSpec((tm, d), lambda i, e: (i, 0)),
        out_shape=jax.ShapeDtypeStruct((n, d), F32),
        scratch_shapes=[pltpu.VMEM((tm, d), BF16), pltpu.VMEM((tm, d), F32)],
        compiler_params=_cparams("parallel", "arbitrary"),
        name="moe_ln",
    )(x, comb, w_gate, w_up, w_down, ln_g, ln_b)


def _rope_tables(pos):
    half = ROT_DIM // 2
    inv = 1.0 / (ROPE_THETA ** (jnp.arange(half, dtype=F32) / half))
    ang = pos.astype(F32)[:, None] * inv[None, :]
    cos, sin = jnp.cos(ang), jnp.sin(ang)
    n = pos.shape[0]
    one = jnp.ones((n, HEAD_DIM - ROT_DIM), F32)
    zero = jnp.zeros((n, HEAD_DIM - ROT_DIM), F32)
    zh = jnp.zeros((n, half), F32)
    rep = LANES // HEAD_DIM
    cos_t = jnp.tile(jnp.concatenate([cos, cos, one], 1), (1, rep))
    sm_t = jnp.tile(jnp.concatenate([-sin, zh, zero], 1), (1, rep))
    sp_t = jnp.tile(jnp.concatenate([zh, sin, zero], 1), (1, rep))
    return cos_t, sm_t, sp_t


def _overlap(nc, nsb, rows):
    i = np.arange(nc)[:, None]
    j = np.arange(nsb)[None, :]
    lo = np.maximum(i * CMP_STRIDE, j * SEL_LEN)
    hi = np.minimum(i * CMP_STRIDE + CMP_LEN, (j + 1) * SEL_LEN)
    ovl = np.zeros((rows, LANES), np.float32)
    ovl[:nc, :nsb] = np.maximum(hi - lo, 0) // CMP_STRIDE
    return jnp.asarray(ovl, BF16)


def _block_diag(blocks):
    n = len(blocks)
    rows = []
    for i, b in enumerate(blocks):
        rows.append(jnp.concatenate([b if j == i else jnp.zeros((b.shape[0], blocks[j].shape[1]), b.dtype)
                                     for j in range(n)], axis=1))
    return jnp.concatenate(rows, axis=0)


def _cmp_weights(w1, pe):
    w_out, pe_out = [], []
    for kv in range(2):
        w_ab, pe_ab = [], []
        for ab in range(2):
            blocks, pes = [], []
            for r in range(CMP_STRIDE):
                rr = ab * CMP_STRIDE + r
                w_r = w1[kv, rr * HEAD_DIM:(rr + 1) * HEAD_DIM, :]
                blocks.append(_block_diag([w_r] * N_KV_HEADS))
                pes.append(jnp.tile(pe[kv, rr], N_KV_HEADS))
            w_ab.append(jnp.concatenate(blocks, axis=0))
            pe_ab.append(jnp.concatenate(pes)[None, :])
        w_out.append(jnp.stack(w_ab))
        pe_out.append(jnp.stack(pe_ab))
    return jnp.stack(w_out).astype(BF16), jnp.stack(pe_out).astype(F32)


def _chunked_kt(k, nck):
    b, t, h, d = k.shape
    return k.transpose(0, 2, 3, 1).reshape(b, h, d, nck, t // nck).transpose(0, 1, 3, 2, 4).astype(BF16)


def _chunked_v(v, nck):
    b, t, h, d = v.shape
    v = v.transpose(0, 2, 1, 3).reshape(b, h, nck, t // nck, d).astype(BF16)
    return jnp.concatenate([v, jnp.ones_like(v)], axis=-1)


def kernel(x_prompt, x_sample, cache_cmp_kv, cache_sel_kv, state_win_kv, state_pool, page_table, w_in, pool_w, pool_scale, gmlp_ln_g, gmlp_ln_b, gmlp_ws, gmlp_bs, cmp_pe, cmp_w1, cmp_b1, cmp_w2, w_branch_pool, w_branch_gmlp, w_branch_nsa, w_out, ln1_g, ln1_b, ln2_g, ln2_b, w_router, b_router, w_gate, w_up, w_down):
    batch, seq, d = x_prompt.shape
    nb, tq, _ = x_sample.shape
    depth = w_in.shape[0]
    n_pool, page = cache_cmp_kv.shape[1:3]
    npages = page_table.shape[1]
    past = npages * page
    keep = state_win_kv.shape[2]
    pw = pool_scale.shape[-1]
    gw = gmlp_ln_g.shape[-1]
    kvw = N_KV_HEADS * HEAD_DIM
    n_q = N_HEADS * HEAD_DIM
    n_kv = 6 * kvw
    n_auv = pw + 2 * gw
    n_gate = 3 * N_HEADS
    alpha = (2 * depth) ** 0.25
    np_rows, ns_rows = batch * seq, nb * tq
    assert pw == gw and seq % KEY_CHUNK == 0 and seq % Q_BLOCK == 0 and tq <= CHUNK and tq % 8 == 0
    assert keep == WINDOW and keep % page == 0 and page % CMP_STRIDE == 0 and np_rows % tq == 0
    nc_p = (seq - CMP_LEN) // CMP_STRIDE + 1
    nc_s = (past + tq - CMP_LEN) // CMP_STRIDE + 1
    assert nc_p < seq // CMP_STRIDE and nc_s < past // CMP_STRIDE
    nsb_p = -(-seq // SEL_LEN)
    nsb_s = -(-(past + tq) // SEL_LEN)
    assert nsb_p <= SEL_ROWS and nsb_s <= LANES

    w_main = w_in[:, :, :n_auv + n_q + n_kv].astype(BF16)
    w_g = jnp.pad(w_in[:, :, n_auv + n_q + n_kv:n_auv + n_q + n_kv + n_gate], ((0, 0), (0, 0), (0, LANES - n_gate))).astype(BF16)
    w_merge = w_in[:, :, n_auv + n_q + n_kv + n_gate:].astype(BF16)
    w_bp, w_bg, w_bn, w_o = (w.astype(BF16) for w in (w_branch_pool, w_branch_gmlp, w_branch_nsa, w_out))
    wg_b, wu_b, wd_b = w_gate.astype(BF16), w_up.astype(BF16), w_down.astype(BF16)
    w_router_t = w_router.T.astype(F32)
    b_router_c = b_router.astype(F32)[:, None]
    grp_w = gw // GMLP_GROUPS

    pos_all = jnp.concatenate([jnp.tile(jnp.arange(seq), batch), jnp.tile(past + jnp.arange(tq), nb)])
    cos_t, sm_t, sp_t = _rope_tables(pos_all)
    ovl_p = _overlap(nc_p, nsb_p, seq // CMP_STRIDE)
    ovl_s = _overlap(nc_s, nsb_s, past // CMP_STRIDE)
    n_keys_s = (npages + 1) * page
    expand_s = jnp.asarray(np.arange(LANES)[:, None] == (np.arange(n_keys_s)[None, :] // SEL_LEN), BF16)
    expand_p = np.where(np.arange(SEL_ROWS)[:, None] == (np.arange(seq)[None, :] // SEL_LEN), NEG_INF, 0.0)
    expand_p = jnp.asarray(expand_p.reshape(SEL_ROWS, seq // KEY_CHUNK, KEY_CHUNK).transpose(1, 0, 2), BF16)

    x_all = jnp.concatenate([x_prompt.reshape(np_rows, d), x_sample.reshape(ns_rows, d)], axis=0)
    rows_last = (0, 1, 3, 4, 5, 2)
    cache_cmp_t = cache_cmp_kv.transpose(rows_last)
    cache_sel_t = cache_sel_kv.transpose(rows_last)
    state_win_t = state_win_kv.transpose(rows_last)
    nck = seq // KEY_CHUNK

    st_p = [[] for _ in range(4)]
    st_s = [[] for _ in range(5)]
    for l in range(depth):
        z_auv, z_q, z_kv, z_g = _proj(x_all, w_main[l], w_g[l], cos_t, sm_t, sp_t, n_auv, n_q, n_kv)

        pool_bd = _block_diag([pool_w[l, g] for g in range(len(POOL_WINDOWS))]).astype(BF16)
        ps, lg, lb = pool_scale[l][None, :], gmlp_ln_g[l][None, :], gmlp_ln_b[l][None, :]
        cl = min(CHUNK, seq)
        bs_rows = jnp.repeat(gmlp_bs[l][:, :cl].T, grp_w, axis=1)
        op_p, og_p = _mix_prompt(z_auv, batch, seq, pool_bd, ps, lg, lb, gmlp_ws[l], bs_rows)
        a_t = z_auv[np_rows:].reshape(nb, tq, n_auv).transpose(1, 0, 2)
        st_t = state_pool[l].transpose(1, 0, 2)
        wrow = jnp.repeat(gmlp_ws[l][:, :tq, :tq].transpose(1, 2, 0), grp_w, axis=2).reshape(tq * tq, gw)
        bsrow = jnp.repeat(gmlp_bs[l][:, :tq].T, grp_w, axis=1)
        op_s, og_s, vn_s = _mix_sample(a_t, st_t, float(past), pool_bd, ps, lg, lb, wrow, bsrow)
        to_rows = lambda y: y.transpose(1, 0, 2).reshape(ns_rows, -1)
        o_pool = jnp.concatenate([op_p, to_rows(op_s)], axis=0)
        o_gmlp = jnp.concatenate([og_p, to_rows(og_s)], axis=0)

        w1_bd, pe_rows = _cmp_weights(cmp_w1[l], cmp_pe[l])
        b1 = cmp_b1[l][:, None, :]
        w2kt = cmp_w2[l, 0].T.astype(BF16)
        w2v = cmp_w2[l, 1].astype(BF16)
        h1_p = _cmp1(z_kv, lambda tr: 0, np_rows, w1_bd, pe_rows)
        kct_p, vc_p = _cmp2_prompt(h1_p.reshape(batch, seq // CMP_STRIDE, -1), b1, w2kt, w2v)
        h1_pool = _cmp1_pages(cache_cmp_t, l, w1_bd, pe_rows)
        kct_s, vc_s = _cmp2_sample(page_table, h1_pool.reshape(n_pool, page // CMP_STRIDE, -1), b1, w2kt, w2v)

        zkv_p = z_kv[:np_rows].reshape(batch, seq, 3, 2, N_KV_HEADS, HEAD_DIM)
        o_p = _nsa_prompt(z_q, z_g, kct_p, vc_p,
                          _chunked_kt(zkv_p[:, :, 1, 0], nck), _chunked_v(zkv_p[:, :, 1, 1], nck),
                          _chunked_kt(zkv_p[:, :, 2, 0], nck), _chunked_v(zkv_p[:, :, 2, 1], nck),
                          ovl_p, expand_p, batch, seq)
        o_s = _nsa_sample(page_table, z_q, z_g, z_kv, np_rows, kct_s, vc_s, ovl_s, expand_s, cache_sel_t, state_win_t,
                          l, tq, past, nsb_s)
        o_nsa = jnp.concatenate([o_p, o_s.astype(BF16)], axis=0)

        x1, comb_t = _merge(x_all, o_pool, o_gmlp, o_nsa, w_merge[l], w_bp[l], w_bg[l], w_bn[l], w_o[l],
                            ln1_g[l][None, :], ln1_b[l][None, :], w_router_t, b_router_c, alpha)
        x_all = _moe(x1, comb_t.T, wg_b, wu_b, wd_b, l, ln2_g[l][None, :], ln2_b[l][None, :], alpha)

        kv_shape = (2, N_KV_HEADS, HEAD_DIM)
        zkv_s = z_kv[np_rows:].reshape(nb, tq, 3, *kv_shape)
        a_p = z_auv[:np_rows, :pw].reshape(batch, seq, pw)
        a_s = z_auv[np_rows:, :pw].reshape(nb, tq, pw)
        zkv_p = zkv_p.reshape(batch, seq, 3, *kv_shape)
        st_p[0].append(zkv_p[:, :, 0])
        st_p[1].append(zkv_p[:, :, 1])
        st_p[2].append(zkv_p[:, seq - min(WINDOW, seq):, 2])
        st_p[3].append(a_p[:, seq - POOL_KEEP:])
        st_s[0].append(zkv_s[:, :, 0])
        st_s[1].append(zkv_s[:, :, 1])
        st_s[2].append(jnp.concatenate([state_win_kv[l], zkv_s[:, :, 2]], axis=1)[:, tq:])
        st_s[3].append(jnp.concatenate([state_pool[l], a_s], axis=1)[:, tq:])
        st_s[4].append(vn_s.transpose(1, 0, 2))

    y_prompt = x_all[:np_rows].reshape(batch, seq, d)
    y_sample = x_all[np_rows:].reshape(nb, tq, d)
    return (y_prompt, y_sample,
            jnp.stack(st_p[0]), jnp.stack(st_p[1]), jnp.stack(st_p[2]), jnp.stack(st_p[3]),
            jnp.stack(st_s[0]), jnp.stack(st_s[1]), jnp.stack(st_s[2]), jnp.stack(st_s[3]), jnp.stack(st_s[4]))
```

```python
import functools
import math

import numpy as np
import jax
import jax.numpy as jnp
from jax import lax
from jax.experimental import pallas as pl
from jax.experimental.pallas import tpu as pltpu

F32 = jnp.float32
BF16 = jnp.bfloat16

POOL_WINDOWS = (2, 4, 8, 16)
POOL_KEEP = max(POOL_WINDOWS) - 1
GMLP_GROUPS = 4
CHUNK = 128
N_HEADS = 8
HEAD_DIM = 64
N_KV_HEADS = 2
GQA = N_HEADS // N_KV_HEADS
ROT_DIM = HEAD_DIM // 4
ROPE_THETA = 500000.0
CMP_LEN = 32
CMP_STRIDE = 16
CMP_HIDDEN = 2 * HEAD_DIM
SEL_LEN = 64
SEL_TOPN = 16
WINDOW = 512
Q_BLOCK = 256
N_EXPERTS = 16
EXPERTS_PER_GROUP = 4
LN_EPS = 1e-5
NEG_INF = -1e30
FORCE = 1e9

LANES = 128
KEY_CHUNK = 512
SEL_ROWS = LANES - HEAD_DIM
VMEM_LIMIT = 56 * 1024 * 1024


def _cparams(*sem):
    return pltpu.CompilerParams(dimension_semantics=sem, vmem_limit_bytes=VMEM_LIMIT)


def _tile(n, cap):
    t = cap
    while n % t:
        t //= 2
    return t


def _dot(a, b):
    return jnp.dot(a, b, preferred_element_type=F32)


def _dot_nt(a, b):
    return lax.dot_general(a, b, (((1,), (1,)), ((), ())), preferred_element_type=F32)


def _layer_norm(x, g, b):
    mu = jnp.mean(x, -1, keepdims=True)
    xc = x - mu
    var = jnp.mean(xc * xc, -1, keepdims=True)
    return xc * lax.rsqrt(var + LN_EPS) * g + b


def _masked_softmax(s, mask):
    sm = jnp.where(mask, s, NEG_INF)
    p = jnp.where(mask, jnp.exp(sm - jnp.max(sm, -1, keepdims=True)), 0.0)
    d = jnp.sum(p, -1, keepdims=True)
    return p / jnp.where(d > 0, d, 1.0)


def _chunked_attention(lhs, rhs_fn, v_ref, bias_fn, c_lo, c_hi, s_sc, mx_sc, acc_sc):
    rows, width = mx_sc.shape
    reps = rows // Q_BLOCK

    mx_sc[...] = jnp.full(mx_sc.shape, -jnp.inf, F32)

    @pl.loop(c_lo, c_hi)
    def _(c):
        s = _dot(lhs, rhs_fn(c)) + jnp.concatenate([bias_fn(c)] * reps, axis=0)
        s_sc[c] = s
        mx = mx_sc[...]
        for j in range(s.shape[1] // width):
            mx = jnp.maximum(mx, s[:, j * width:(j + 1) * width])
        mx_sc[...] = mx

    m = jnp.max(mx_sc[...], -1, keepdims=True)
    acc_sc[...] = jnp.zeros(acc_sc.shape, F32)

    @pl.loop(c_lo, c_hi)
    def _(c):
        p = jnp.exp(s_sc[c] - m).astype(BF16)
        acc_sc[...] += _dot(p, v_ref[c])

    acc = acc_sc[...]
    return acc / pltpu.roll(acc, acc.shape[1] // 2, 1)


def _topk_rows(score_t, k, n_valid):
    nq = score_t.shape[1]
    sub_rows = 8
    nv = -(-n_valid // sub_rows)
    tiles = [score_t[sub_rows * v:sub_rows * (v + 1)] for v in range(nv)]
    sub = lax.broadcasted_iota(jnp.int32, (sub_rows, nq), 0)
    ranks = [jnp.zeros((sub_rows, nq), F32) for _ in range(nv)]
    for i in range(n_valid):
        vi, si = divmod(i, sub_rows)
        r = jnp.broadcast_to(score_t[i:i + 1], (sub_rows, nq))
        for v in range(nv):
            if v < vi:
                ahead = r > tiles[v]
            elif v > vi:
                ahead = r >= tiles[v]
            else:
                ahead = (r > tiles[v]) | ((r >= tiles[v]) & (sub > si))
            ranks[v] = ranks[v] + jnp.where(ahead, 1.0, 0.0)
    sel = [jnp.where((ranks[v] < k) & (sub_rows * v + sub < n_valid), 1.0, 0.0) for v in range(nv)]
    pad = score_t.shape[0] - sub_rows * nv
    if pad:
        sel.append(jnp.zeros((pad, nq), F32))
    return jnp.concatenate(sel, axis=0)


def _importance(p_cmp, rows, ovl):
    psum = p_cmp[0:rows]
    for g in range(1, GQA):
        psum = psum + p_cmp[g * rows:(g + 1) * rows]
    hi = psum.astype(BF16)
    lo = (psum - hi.astype(F32)).astype(BF16)
    return _dot(hi, ovl) + _dot(lo, ovl)


def _select_blocks(imp, qpos_t, nsb):
    imp_t = imp.T
    blk = lax.broadcasted_iota(jnp.int32, imp_t.shape, 0)
    cur = jnp.right_shift(qpos_t, int(math.log2(SEL_LEN)))
    forced = (blk == 0) | (blk == cur) | (blk == cur - 1)
    score_t = jnp.where(forced, FORCE, jnp.where(blk <= cur, imp_t, -FORCE))
    return _topk_rows(score_t, min(SEL_TOPN, nsb), nsb).T


def _head_gate(gs, col):
    lane = lax.broadcasted_iota(jnp.int32, gs.shape, 1)
    return jnp.sum(jnp.where(lane == col, gs, 0.0), -1, keepdims=True)


def _proj_kernel(*refs, n_auv, n_q, n_kv, prompt_tiles):
    x_refs, (w_ref, wg_ref, cos_ref, sm_ref, sp_ref, auv_ref, q_ref, kv_ref, g_ref) = refs[:-9], refs[-9:]
    x = x_refs[0][...]
    if len(x_refs) == 2:
        x = jnp.where(pl.program_id(0) < prompt_tiles, x, x_refs[1][...])
    xb = x.astype(BF16)
    z = _dot(xb, w_ref[...])
    auv_ref[...] = z[:, :n_auv]
    cos, sm, sp = cos_ref[...], sm_ref[...], sp_ref[...]
    half = ROT_DIM // 2

    def rope(t):
        return t * cos + pltpu.roll(t, LANES - half, 1) * sm + pltpu.roll(t, half, 1) * sp

    for j in range(n_q // LANES):
        c0 = n_auv + j * LANES
        q_ref[:, j * LANES:(j + 1) * LANES] = rope(z[:, c0:c0 + LANES]) * (HEAD_DIM ** -0.5)
    for j in range(n_kv // LANES):
        c0 = n_auv + n_q + j * LANES
        t = z[:, c0:c0 + LANES]
        kv_ref[:, j * LANES:(j + 1) * LANES] = rope(t) if j % 2 == 0 else t
    g_ref[...] = _dot(xb, wg_ref[...])


def _proj(xs, w_main, w_gate, cos, sm, sp, n_auv, n_q, n_kv):
    d = xs[0].shape[1]
    n = sum(x.shape[0] for x in xs)
    tm = _tile(math.gcd(*[x.shape[0] for x in xs]), 512)
    p_tiles = xs[0].shape[0] // tm
    row = lambda i: (i, 0)
    fixed = lambda i: (0, 0)
    if len(xs) == 1:
        x_specs = [pl.BlockSpec((tm, d), row)]
    else:
        x_specs = [pl.BlockSpec((tm, d), lambda i: (jnp.minimum(i, p_tiles - 1), 0)),
                   pl.BlockSpec((tm, d), lambda i: (jnp.maximum(i - p_tiles, 0), 0))]
    return pl.pallas_call(
        functools.partial(_proj_kernel, n_auv=n_auv, n_q=n_q, n_kv=n_kv, prompt_tiles=p_tiles),
        grid=(n // tm,),
        in_specs=x_specs + [
                  pl.BlockSpec(w_main.shape, fixed),
                  pl.BlockSpec(w_gate.shape, fixed),
                  pl.BlockSpec((tm, LANES), row), pl.BlockSpec((tm, LANES), row), pl.BlockSpec((tm, LANES), row)],
        out_specs=[pl.BlockSpec((tm, n_auv), row), pl.BlockSpec((tm, n_q), row),
                   pl.BlockSpec((tm, n_kv), row), pl.BlockSpec((tm, LANES), row)],
        out_shape=[jax.ShapeDtypeStruct((n, n_auv), F32), jax.ShapeDtypeStruct((n, n_q), F32),
                   jax.ShapeDtypeStruct((n, n_kv), F32), jax.ShapeDtypeStruct((n, LANES), F32)],
        compiler_params=_cparams("parallel"),
        name="in_proj",
    )(*xs, w_main, w_gate, cos, sm, sp)


def _pool_select(grp, vals):
    out = vals[-1]
    for gi in range(len(vals) - 2, -1, -1):
        out = jnp.where(grp == gi, vals[gi], out)
    return out


def _mix_prompt_kernel(auv_ref, halo_ref, pw_ref, ps_ref, lng_ref, lnb_ref, ws_ref, bsr_ref, op_ref, og_ref, *, tm, pw):
    i = pl.program_id(1)
    halo_rows = halo_ref.shape[0]
    a = auv_ref[:, 0:pw]
    u = auv_ref[:, pw:2 * pw]
    v = auv_ref[:, 2 * pw:3 * pw]
    halo = jnp.where(i == 0, 0.0, halo_ref[...])
    ext = jnp.concatenate([halo, a], axis=0)
    sums, cur, width = [], ext, 1
    for win in POOL_WINDOWS:
        while width < win:
            cur = cur + pltpu.roll(cur, width, 0)
            width *= 2
        sums.append(cur[halo_rows:])
    lane = lax.broadcasted_iota(jnp.int32, (tm, pw), 1)
    grp = lane // (pw // len(POOL_WINDOWS))
    pos = (i * tm + lax.broadcasted_iota(jnp.int32, (tm, pw), 0)).astype(F32)
    total = _pool_select(grp, sums)
    win = _pool_select(grp, [jnp.full((tm, pw), float(w), F32) for w in POOL_WINDOWS])
    d = total / jnp.minimum(win, pos + 1.0) - a
    op_ref[...] = (_dot(d.astype(BF16), pw_ref[...]) * ps_ref[...]).astype(op_ref.dtype)

    vn = _layer_norm(v, lng_ref[...], lnb_ref[...]).astype(BF16)
    r = lax.broadcasted_iota(jnp.int32, (CHUNK, CHUNK), 0)
    c = lax.broadcasted_iota(jnp.int32, (CHUNK, CHUNK), 1)
    wms = [jnp.where(c <= r, ws_ref[g], 0.0).astype(BF16) for g in range(GMLP_GROUPS)]
    grp_c = lax.broadcasted_iota(jnp.int32, (CHUNK, pw), 1) // (pw // GMLP_GROUPS)
    for j in range(tm // CHUNK):
        rows = slice(j * CHUNK, (j + 1) * CHUNK)
        vc = vn[rows]
        mix = _pool_select(grp_c, [_dot(wms[g], vc) for g in range(GMLP_GROUPS)])
        og_ref[rows, :] = (u[rows] * (mix + bsr_ref[...])).astype(og_ref.dtype)


def _mix_prompt(z_auv, batch, seq, pool_bd, pool_scale, ln_g, ln_b, ws, bs_rows):
    pw = pool_scale.shape[-1]
    tm = _tile(seq, 512)
    nt = seq // tm
    halo = 2 * 8
    hb = tm // halo
    fixed2 = lambda b, i: (0, 0)
    return pl.pallas_call(
        functools.partial(_mix_prompt_kernel, tm=tm, pw=pw),
        grid=(batch, nt),
        in_specs=[pl.BlockSpec((tm, 3 * pw), lambda b, i: (b * nt + i, 0)),
                  pl.BlockSpec((halo, pw), lambda b, i: (jnp.maximum((b * nt + i) * hb - 1, 0), 0)),
                  pl.BlockSpec((pw, pw), fixed2), pl.BlockSpec((1, pw), fixed2),
                  pl.BlockSpec((1, pw), fixed2), pl.BlockSpec((1, pw), fixed2),
                  pl.BlockSpec(ws.shape, lambda b, i: (0, 0, 0)),
                  pl.BlockSpec((CHUNK, pw), fixed2)],
        out_specs=[pl.BlockSpec((tm, pw), lambda b, i: (b * nt + i, 0)),
                   pl.BlockSpec((tm, pw), lambda b, i: (b * nt + i, 0))],
        out_shape=[jax.ShapeDtypeStruct((batch * seq, pw), BF16), jax.ShapeDtypeStruct((batch * seq, pw), BF16)],
        compiler_params=_cparams("parallel", "parallel"),
        name="mix_prompt",
    )(z_auv, z_auv, pool_bd, pool_scale, ln_g, ln_b, ws, bs_rows)


def _mix_sample_kernel(a_ref, st_ref, pw_ref, ps_ref, lng_ref, lnb_ref, wrow_ref, bsr_ref, op_ref, og_ref, vn_ref,
                       *, tq, keep, pos0, pw):
    nb = a_ref.shape[1]
    ext = [st_ref[i] for i in range(keep)] + [a_ref[t][:, 0:pw] for t in range(tq)]
    grp = lax.broadcasted_iota(jnp.int32, (nb, pw), 1) // (pw // len(POOL_WINDOWS))
    for t in range(tq):
        e = keep + t
        acc, k, sums = ext[e], 1, []
        for win in POOL_WINDOWS:
            while k < win:
                if e - k >= 0:
                    acc = acc + ext[e - k]
                k += 1
            sums.append(acc)
        total = _pool_select(grp, sums)
        cnt = _pool_select(grp, [jnp.full((nb, pw), min(float(w), pos0 + t + 1.0), F32) for w in POOL_WINDOWS])
        d = total / cnt - ext[e]
        op_ref[t] = (_dot(d.astype(BF16), pw_ref[...]) * ps_ref[...]).astype(op_ref.dtype)
    vns = []
    for t in range(tq):
        vn = _layer_norm(a_ref[t][:, 2 * pw:3 * pw], lng_ref[...], lnb_ref[...])
        vn_ref[t] = vn
        vns.append(vn)
    for t in range(tq):
        mix = bsr_ref[t:t + 1, :]
        for s in range(t + 1):
            mix = mix + wrow_ref[t * tq + s:t * tq + s + 1, :] * vns[s]
        og_ref[t] = (a_ref[t][:, pw:2 * pw] * mix).astype(og_ref.dtype)


def _mix_sample(a_t, st_t, pos0, pool_bd, pool_scale, ln_g, ln_b, wrow, bsrow):
    tq, nb, w3 = a_t.shape
    keep = st_t.shape[0]
    pw = w3 // 3
    full = lambda shape: pl.BlockSpec(shape, lambda i: (0,) * len(shape))
    return pl.pallas_call(
        functools.partial(_mix_sample_kernel, tq=tq, keep=keep, pos0=pos0, pw=pw),
        grid=(1,),
        in_specs=[full(a_t.shape), full(st_t.shape), full(pool_bd.shape), full(pool_scale.shape),
                  full(ln_g.shape), full(ln_b.shape), full(wrow.shape), full(bsrow.shape)],
        out_specs=[full((tq, nb, pw)), full((tq, nb, pw)), full((tq, nb, pw))],
        out_shape=[jax.ShapeDtypeStruct((tq, nb, pw), BF16), jax.ShapeDtypeStruct((tq, nb, pw), BF16),
                   jax.ShapeDtypeStruct((tq, nb, pw), F32)],
        compiler_params=_cparams("arbitrary"),
        name="mix_sample",
    )(a_t, st_t, pool_bd, pool_scale, ln_g, ln_b, wrow, bsrow)


def _cmp1_kernel(x_ref, w_ref, pe_ref, h_ref, *, g):
    hw = N_KV_HEADS * CMP_HIDDEN
    xcat = jnp.concatenate([x_ref[pl.ds(r, g, stride=CMP_STRIDE), :] for r in range(CMP_STRIDE)], axis=1)
    for ab in range(2):
        lhs = (xcat + pe_ref[ab]).astype(BF16)
        h_ref[:, ab * hw:(ab + 1) * hw] = _dot(lhs, w_ref[ab])


def _cmp1(x2d, row_block0, n_rows, w1_bd, pe_rows):
    tr = _tile(n_rows, 4096)
    g = tr // CMP_STRIDE
    kvw = N_KV_HEADS * HEAD_DIM
    hcols = 2 * N_KV_HEADS * CMP_HIDDEN
    return pl.pallas_call(
        functools.partial(_cmp1_kernel, g=g),
        grid=(n_rows // tr, 2),
        in_specs=[pl.BlockSpec((tr, kvw), lambda i, kv: (row_block0(tr) + i, kv)),
                  pl.BlockSpec((None,) + w1_bd.shape[1:], lambda i, kv: (kv, 0, 0, 0)),
                  pl.BlockSpec((None,) + pe_rows.shape[1:], lambda i, kv: (kv, 0, 0, 0))],
        out_specs=pl.BlockSpec((g, hcols), lambda i, kv: (i, kv)),
        out_shape=jax.ShapeDtypeStruct((n_rows // CMP_STRIDE, 2 * hcols), F32),
        compiler_params=_cparams("parallel", "parallel"),
        name="cmp_stage1",
    )(x2d, w1_bd, pe_rows)


def _cmp1_pages_kernel(xt_ref, w_ref, pe_ref, h_ref, *x_scs, g):
    n_pages, n_h, hd, page = xt_ref.shape
    parts = len(x_scs)
    pp, gp = n_pages // parts, g // parts
    hw = N_KV_HEADS * CMP_HIDDEN
    for k, x_sc in enumerate(x_scs):
        for p in range(pp):
            x_sc[p * page:(p + 1) * page, :] = xt_ref[k * pp + p].reshape(n_h * hd, page).T
        xcat = jnp.concatenate([x_sc[pl.ds(r, gp, stride=CMP_STRIDE), :] for r in range(CMP_STRIDE)], axis=1)
        for ab in range(2):
            lhs = (xcat + pe_ref[ab]).astype(BF16)
            h_ref[k * gp:(k + 1) * gp, ab * hw:(ab + 1) * hw] = _dot(lhs, w_ref[ab])


def _cmp1_pages(cache_t, layer, w1_bd, pe_rows):
    _, n_pool, _, n_h, hd, page = cache_t.shape
    pp = _tile(n_pool, 8192 // page)
    g = pp * page // CMP_STRIDE
    parts = 2 if pp % 2 == 0 else 1
    hcols = 2 * N_KV_HEADS * CMP_HIDDEN
    return pl.pallas_call(
        functools.partial(_cmp1_pages_kernel, g=g),
        grid=(n_pool // pp, 2),
        in_specs=[pl.BlockSpec((None, pp, None, n_h, hd, page), lambda i, kv: (layer, i, kv, 0, 0, 0)),
                  pl.BlockSpec((None,) + w1_bd.shape[1:], lambda i, kv: (kv, 0, 0, 0)),
                  pl.BlockSpec((None,) + pe_rows.shape[1:], lambda i, kv: (kv, 0, 0, 0))],
        out_specs=pl.BlockSpec((g, hcols), lambda i, kv: (i, kv)),
        out_shape=jax.ShapeDtypeStruct((n_pool * page // CMP_STRIDE, 2 * hcols), F32),
        scratch_shapes=[pltpu.VMEM((pp // parts * page, n_h * hd), F32)] * parts,
        compiler_params=_cparams("parallel", "parallel"),
        name="cmp_stage1_pages",
    )(cache_t, w1_bd, pe_rows)


def _gelu_tanh(x):
    return 0.5 * x * (1.0 + jnp.tanh(math.sqrt(2.0 / math.pi) * (x + 0.044715 * (x * x * x))))


def _cmp2_body(h_refs, b1_ref, w2kt_ref, w2v_ref, kct_ref, vc_ref):
    hall = jnp.concatenate([r[...] for r in h_refs], axis=0) if len(h_refs) > 1 else h_refs[0][...]
    g = hall.shape[0]
    hw = N_KV_HEADS * CMP_HIDDEN
    for kv in range(2):
        for h in range(N_KV_HEADS):
            ca = (kv * 2) * hw + h * CMP_HIDDEN
            cb = (kv * 2 + 1) * hw + h * CMP_HIDDEN
            hid = hall[:, ca:ca + CMP_HIDDEN] + pltpu.roll(hall[:, cb:cb + CMP_HIDDEN], g - 1, 0) + b1_ref[kv]
            hid = _gelu_tanh(hid).astype(BF16)
            if kv == 0:
                kct_ref[h] = _dot_nt(w2kt_ref[...], hid).astype(kct_ref.dtype)
            else:
                vc_ref[h] = _dot(hid, w2v_ref[...]).astype(vc_ref.dtype)


def _cmp2_prompt_kernel(h_ref, b1_ref, w2kt_ref, w2v_ref, kct_ref, vc_ref):
    _cmp2_body([h_ref], b1_ref, w2kt_ref, w2v_ref, kct_ref, vc_ref)


def _cmp2_sample_kernel(pt_ref, *refs, npages):
    _cmp2_body(list(refs[:npages]), *refs[npages:])


def _cmp2_out(n, g):
    specs = [pl.BlockSpec((None, N_KV_HEADS, HEAD_DIM, g), lambda b, *_: (b, 0, 0, 0)),
             pl.BlockSpec((None, N_KV_HEADS, g, HEAD_DIM), lambda b, *_: (b, 0, 0, 0))]
    shapes = [jax.ShapeDtypeStruct((n, N_KV_HEADS, HEAD_DIM, g), BF16),
              jax.ShapeDtypeStruct((n, N_KV_HEADS, g, HEAD_DIM), BF16)]
    return specs, shapes


def _cmp2_prompt(h1, b1, w2kt, w2v):
    n, g, hc = h1.shape
    out_specs, out_shape = _cmp2_out(n, g)
    return pl.pallas_call(
        _cmp2_prompt_kernel,
        grid=(n,),
        in_specs=[pl.BlockSpec((None, g, hc), lambda b: (b, 0, 0)),
                  pl.BlockSpec(b1.shape, lambda b: (0, 0, 0)),
                  pl.BlockSpec(w2kt.shape, lambda b: (0, 0)),
                  pl.BlockSpec(w2v.shape, lambda b: (0, 0))],
        out_specs=out_specs, out_shape=out_shape,
        compiler_params=_cparams("parallel"),
        name="cmp_stage2_prompt",
    )(h1, b1, w2kt, w2v)


def _cmp2_sample(page_table, h1_pool, b1, w2kt, w2v):
    n, npages = page_table.shape
    _, gp, hc = h1_pool.shape
    out_specs, out_shape = _cmp2_out(n, npages * gp)
    page_specs = [pl.BlockSpec((None, gp, hc), functools.partial(lambda b, pt, j: (pt[b, j], 0, 0), j=j))
                  for j in range(npages)]
    return pl.pallas_call(
        functools.partial(_cmp2_sample_kernel, npages=npages),
        grid_spec=pltpu.PrefetchScalarGridSpec(
            num_scalar_prefetch=1, grid=(n,),
            in_specs=page_specs + [pl.BlockSpec(b1.shape, lambda b, pt: (0, 0, 0)),
                                   pl.BlockSpec(w2kt.shape, lambda b, pt: (0, 0)),
                                   pl.BlockSpec(w2v.shape, lambda b, pt: (0, 0))],
            out_specs=out_specs),
        out_shape=out_shape,
        compiler_params=_cparams("parallel"),
        name="cmp_stage2_sample",
    )(page_table, *([h1_pool] * npages), b1, w2kt, w2v)


def _nsa_prompt_kernel(q_ref, g_ref, kct_ref, vc_ref, kts_ref, vs_ref, ktw_ref, vw_ref, ovl_ref, exp_ref, o_ref,
                       s_sc, mx_sc, acc_sc, *, nsb):
    h = pl.program_id(1)
    qb = pl.program_id(2)
    tq = Q_BLOCK
    rows = GQA * tq
    t0 = qb * tq
    qf = q_ref[...]
    q = jnp.concatenate([qf[:, g * HEAD_DIM:(g + 1) * HEAD_DIM] for g in range(GQA)], axis=0).astype(BF16)
    qpos_r = t0 + (lax.broadcasted_iota(jnp.int32, (rows, 1), 0) & (tq - 1))

    ncp = kct_ref.shape[-1]
    s = _dot(q, kct_ref[...])
    cend = lax.broadcasted_iota(jnp.int32, (1, ncp), 1) * CMP_STRIDE + (CMP_LEN - 1)
    p_cmp = _masked_softmax(s, cend <= qpos_r)
    o_cmp = _dot(p_cmp.astype(BF16), vc_ref[...])

    imp = _importance(p_cmp, tq, ovl_ref[...])
    qpos_t = t0 + lax.broadcasted_iota(jnp.int32, (imp.shape[1], imp.shape[0]), 1)
    selm = _select_blocks(imp, qpos_t, nsb)
    qpos_c = t0 + lax.broadcasted_iota(jnp.int32, (tq, 1), 0)
    kcol = lax.broadcasted_iota(jnp.int32, (1, KEY_CHUNK), 1)

    hidden = jnp.concatenate([1.0 - selm[:, :SEL_ROWS]] * GQA, axis=0).astype(BF16)
    q_sel = jnp.concatenate([q, hidden], axis=1)

    def sel_rhs(c):
        return jnp.concatenate([kts_ref[c], exp_ref[c]], axis=0)

    def sel_bias(c):
        return jnp.where(c * KEY_CHUNK + kcol <= qpos_c, 0.0, NEG_INF)

    def win_bias(c):
        kpos = c * KEY_CHUNK + kcol
        return jnp.where((kpos <= qpos_c) & (kpos > qpos_c - WINDOW), 0.0, NEG_INF)

    c_hi = (t0 + tq - 1) // KEY_CHUNK + 1
    c_lo = jnp.maximum(t0 - WINDOW + 1, 0) // KEY_CHUNK
    o_sel = _chunked_attention(q_sel, sel_rhs, vs_ref, sel_bias, 0, c_hi, s_sc, mx_sc, acc_sc)
    o_win = _chunked_attention(q, lambda c: ktw_ref[c], vw_ref, win_bias, c_lo, c_hi, s_sc, mx_sc, acc_sc)

    gs = jax.nn.sigmoid(g_ref[...])
    outs = []
    for g in range(GQA):
        r = slice(g * tq, (g + 1) * tq)
        col = (h * GQA + g) * 3
        outs.append(_head_gate(gs, col) * o_cmp[r] + _head_gate(gs, col + 1) * o_sel[r, :HEAD_DIM]
                    + _head_gate(gs, col + 2) * o_win[r, :HEAD_DIM])
    o_ref[...] = jnp.concatenate(outs, axis=1).astype(o_ref.dtype)


def _nsa_prompt(z_q, z_g, kct, vc, kts, vs, ktw, vw, ovl, expand, batch, seq):
    nqb = seq // Q_BLOCK
    hq = GQA * HEAD_DIM
    nsb = -(-seq // SEL_LEN)
    per_head = lambda shape: pl.BlockSpec((None, None) + shape, lambda b, h, i: (b, h) + (0,) * len(shape))
    return pl.pallas_call(
        functools.partial(_nsa_prompt_kernel, nsb=nsb),
        grid=(batch, N_KV_HEADS, nqb),
        in_specs=[pl.BlockSpec((Q_BLOCK, hq), lambda b, h, i: (b * nqb + i, h)),
                  pl.BlockSpec((Q_BLOCK, LANES), lambda b, h, i: (b * nqb + i, 0)),
                  per_head(kct.shape[2:]), per_head(vc.shape[2:]),
                  per_head(kts.shape[2:]), per_head(vs.shape[2:]),
                  per_head(ktw.shape[2:]), per_head(vw.shape[2:]),
                  pl.BlockSpec(ovl.shape, lambda b, h, i: (0, 0)),
                  pl.BlockSpec(expand.shape, lambda b, h, i: (0, 0, 0))],
        out_specs=pl.BlockSpec((Q_BLOCK, hq), lambda b, h, i: (b * nqb + i, h)),
        out_shape=jax.ShapeDtypeStruct((batch * seq, N_KV_HEADS * hq), BF16),
        scratch_shapes=[pltpu.VMEM((seq // KEY_CHUNK, GQA * Q_BLOCK, KEY_CHUNK), F32),
                        pltpu.VMEM((GQA * Q_BLOCK, LANES), F32), pltpu.VMEM((GQA * Q_BLOCK, vs.shape[-1]), F32)],
        compiler_params=_cparams("parallel", "parallel", "parallel"),
        name="nsa_prompt",
    )(z_q, z_g, kct, vc, kts, vs, ktw, vw, ovl, expand)


def _nsa_sample_kernel(pt_ref, q_ref, g_ref, kvn_ref, kct_ref, vc_ref, ovl_ref, exp_ref, win_ref, *rest,
                       npages, tq, pos0, nsb):
    page_refs = rest[:npages]
    o_ref = rest[npages]
    newpg = rest[npages + 1]
    page = newpg.shape[0]
    kvw = N_KV_HEADS * HEAD_DIM
    newpg[...] = jnp.zeros(newpg.shape, newpg.dtype)
    newpg[0:tq, :] = kvn_ref[...]
    rows = GQA * tq
    qpos_r = pos0 + lax.rem(lax.broadcasted_iota(jnp.int32, (rows, 1), 0), tq)
    qf = q_ref[...]
    gs = jax.nn.sigmoid(g_ref[...])
    keep = win_ref.shape[-1]
    heads = range(N_KV_HEADS)
    qs = [jnp.concatenate([qf[:, (h * GQA + g) * HEAD_DIM:(h * GQA + g + 1) * HEAD_DIM] for g in range(GQA)],
                          axis=0).astype(BF16) for h in heads]

    ncp = kct_ref.shape[-1]
    cend = lax.broadcasted_iota(jnp.int32, (1, ncp), 1) * CMP_STRIDE + (CMP_LEN - 1)
    p_cmps = [_masked_softmax(_dot(qs[h], kct_ref[h]), cend <= qpos_r) for h in heads]
    o_cmps = [_dot(p_cmps[h].astype(BF16), vc_ref[h]) for h in heads]

    imps = [_importance(p_cmps[h], tq, ovl_ref[...]) for h in heads]
    imp = jnp.concatenate(imps + [jnp.zeros((LANES - N_KV_HEADS * tq, LANES), F32)], axis=0)
    qpos_t = pos0 + lax.rem(lax.broadcasted_iota(jnp.int32, (LANES, LANES), 1), tq)
    selm_all = _select_blocks(imp, qpos_t, nsb)

    outs = []
    for h in heads:
        q, o_cmp = qs[h], o_cmps[h]
        kc0 = h * HEAD_DIM
        vc0 = kvw + h * HEAD_DIM

        selm = selm_all[h * tq:(h + 1) * tq]
        e = _dot(jnp.concatenate([selm] * GQA, axis=0).astype(BF16), exp_ref[...])
        k_new = newpg[:, 2 * kvw + kc0:2 * kvw + kc0 + HEAD_DIM].astype(BF16)
        v_new = newpg[:, 2 * kvw + vc0:2 * kvw + vc0 + HEAD_DIM].astype(BF16)
        s = jnp.concatenate([_dot(q, page_refs[p][0, h].astype(BF16)) for p in range(npages)] + [_dot_nt(q, k_new)], axis=1)
        kpos = lax.broadcasted_iota(jnp.int32, (1, s.shape[1]), 1)
        p = _masked_softmax(s, (e > 0.5) & (kpos <= qpos_r)).astype(BF16)
        o_sel = _dot(p[:, npages * page:], v_new)
        for j in range(npages):
            o_sel = o_sel + _dot_nt(p[:, j * page:(j + 1) * page], page_refs[j][1, h].astype(BF16))

        k_new = newpg[:, 4 * kvw + kc0:4 * kvw + kc0 + HEAD_DIM].astype(BF16)
        v_new = newpg[:, 4 * kvw + vc0:4 * kvw + vc0 + HEAD_DIM].astype(BF16)
        s = jnp.concatenate([_dot(q, win_ref[0, h].astype(BF16)), _dot_nt(q, k_new)], axis=1)
        kpos = pos0 - keep + lax.broadcasted_iota(jnp.int32, (1, s.shape[1]), 1)
        p = _masked_softmax(s, (kpos <= qpos_r) & (kpos > qpos_r - WINDOW)).astype(BF16)
        o_win = _dot_nt(p[:, :keep], win_ref[1, h].astype(BF16)) + _dot(p[:, keep:], v_new)

        for g in range(GQA):
            r = slice(g * tq, (g + 1) * tq)
            col = (h * GQA + g) * 3
            outs.append(_head_gate(gs, col) * o_cmp[r] + _head_gate(gs, col + 1) * o_sel[r]
                        + _head_gate(gs, col + 2) * o_win[r])
    o_ref[...] = jnp.concatenate(outs, axis=1).astype(o_ref.dtype)


def _nsa_sample(page_table, z_q, z_g, z_kv, row0, kct, vc, ovl, expand, cache_sel, state_win, layer, tq, pos0, nsb):
    n, npages = page_table.shape
    page = cache_sel.shape[-1]
    blk0 = row0 // tq
    hq = z_q.shape[1]
    page_specs = [pl.BlockSpec((None, None) + cache_sel.shape[2:],
                               functools.partial(lambda b, pt, j: (layer, pt[b, j], 0, 0, 0, 0), j=j))
                  for j in range(npages)]
    return pl.pallas_call(
        functools.partial(_nsa_sample_kernel, npages=npages, tq=tq, pos0=pos0, nsb=nsb),
        grid_spec=pltpu.PrefetchScalarGridSpec(
            num_scalar_prefetch=1, grid=(n,),
            in_specs=[pl.BlockSpec((tq, hq), lambda b, pt: (blk0 + b, 0)),
                      pl.BlockSpec((tq, LANES), lambda b, pt: (blk0 + b, 0)),
                      pl.BlockSpec((tq, z_kv.shape[1]), lambda b, pt: (blk0 + b, 0)),
                      pl.BlockSpec((None,) + kct.shape[1:], lambda b, pt: (b, 0, 0, 0)),
                      pl.BlockSpec((None,) + vc.shape[1:], lambda b, pt: (b, 0, 0, 0)),
                      pl.BlockSpec(ovl.shape, lambda b, pt: (0, 0)),
                      pl.BlockSpec(expand.shape, lambda b, pt: (0, 0)),
                      pl.BlockSpec((None, None) + state_win.shape[2:], lambda b, pt: (layer, b, 0, 0, 0, 0))] + page_specs,
            out_specs=pl.BlockSpec((tq, hq), lambda b, pt: (b, 0)),
            scratch_shapes=[pltpu.VMEM((page, z_kv.shape[1]), F32)]),
        out_shape=jax.ShapeDtypeStruct((n * tq, hq), F32),
        compiler_params=_cparams("parallel"),
        name="nsa_sample",
    )(page_table, z_q, z_g, z_kv, kct, vc, ovl, expand, state_win, *([cache_sel] * npages))


def _route(x1, wrt, br, ct_ref):
    logits = lax.dot_general(wrt, x1, (((1,), (1,)), ((), ())), precision=lax.Precision.HIGHEST,
                             preferred_element_type=F32)
    aff = jax.nn.sigmoid(logits)
    score = aff + br
    sc = [score[e:e + 1, :] for e in range(N_EXPERTS)]
    af = [aff[e:e + 1, :] for e in range(N_EXPERTS)]
    n_grp = N_EXPERTS // EXPERTS_PER_GROUP
    gsum = []
    for k in range(n_grp):
        m = sc[k * EXPERTS_PER_GROUP:(k + 1) * EXPERTS_PER_GROUP]
        best = None
        for i in range(EXPERTS_PER_GROUP):
            for j in range(i + 1, EXPERTS_PER_GROUP):
                pair = m[i] + m[j]
                best = pair if best is None else jnp.maximum(best, pair)
        gsum.append(best)
    is_best, taken = [], None
    for k in range(n_grp):
        ok = None
        for j in range(k + 1, n_grp):
            c = gsum[k] >= gsum[j]
            ok = c if ok is None else ok & c
        if ok is None:
            ok = jnp.ones_like(gsum[k], dtype=jnp.bool_)
        if taken is not None:
            ok = ok & jnp.logical_not(taken)
        is_best.append(ok)
        taken = ok if taken is None else taken | ok
    sel = []
    for e in range(N_EXPERTS):
        k = e // EXPERTS_PER_GROUP
        rank = jnp.zeros_like(sc[e])
        for e2 in range(k * EXPERTS_PER_GROUP, (k + 1) * EXPERTS_PER_GROUP):
            if e2 == e:
                continue
            ahead = (sc[e2] >= sc[e]) if e2 < e else (sc[e2] > sc[e])
            rank = rank + ahead.astype(F32)
        sel.append(is_best[k] & (rank < 2.0))
    denom = jnp.zeros_like(sc[0])
    for e in range(N_EXPERTS):
        denom = denom + jnp.where(sel[e], af[e], 0.0)
    for e in range(N_EXPERTS):
        ct_ref[e:e + 1, :] = jnp.where(sel[e], af[e] / denom, 0.0)


def _merge_kernel(*refs, alpha, prompt_tiles, n_x):
    x_refs = refs[:n_x]
    (opp_ref, ops_ref, ogp_ref, ogs_ref, onp_ref, ons_ref, wm_ref, wp_ref, wgm_ref, wn_ref, wo_ref,
     lg_ref, lb_ref, wrt_ref, br_ref, x1_ref, ct_ref) = refs[n_x:]
    from_prompt = pl.program_id(0) < prompt_tiles
    x = x_refs[0][...]
    if n_x == 2:
        x = jnp.where(from_prompt, x, x_refs[1][...])
    xb = x.astype(BF16)
    d = x.shape[1]
    merged = None
    for b, (p_r, s_r, w_r) in enumerate(((opp_ref, ops_ref, wp_ref), (ogp_ref, ogs_ref, wgm_ref), (onp_ref, ons_ref, wn_ref))):
        gate = jax.nn.sigmoid(_dot(xb, wm_ref[:, b * d:(b + 1) * d]))
        term = gate * _dot(jnp.where(from_prompt, p_r[...], s_r[...]), w_r[...])
        merged = term if merged is None else merged + term
    out = _dot(merged.astype(BF16), wo_ref[...])
    x1 = _layer_norm(alpha * x + out, lg_ref[...], lb_ref[...])
    x1_ref[...] = x1
    _route(x1, wrt_ref[...], br_ref[...], ct_ref)


def _merge(xs, o_pool, o_gmlp, o_nsa, w_merge, w_bp, w_bg, w_bn, w_out, ln_g, ln_b, w_router_t, b_router, alpha):
    d = xs[0].shape[1]
    n = sum(x.shape[0] for x in xs)
    n_p, n_s = o_pool[0].shape[0], o_pool[1].shape[0]
    tm = _tile(math.gcd(n_p, n_s), 512)
    p_tiles = n_p // tm
    row = lambda i: (i, 0)
    p_row = lambda i: (jnp.minimum(i, p_tiles - 1), 0)
    s_row = lambda i: (jnp.maximum(i - p_tiles, 0), 0)
    fixed = lambda a: pl.BlockSpec(a.shape, lambda i: (0, 0))
    pair_specs, pair_args = [], []
    for o_p, o_s in (o_pool, o_gmlp, o_nsa):
        pair_specs += [pl.BlockSpec((tm, o_p.shape[1]), p_row), pl.BlockSpec((tm, o_s.shape[1]), s_row)]
        pair_args += [o_p, o_s]
    return pl.pallas_call(
        functools.partial(_merge_kernel, alpha=alpha, prompt_tiles=p_tiles, n_x=len(xs)),
        grid=(n // tm,),
        in_specs=([pl.BlockSpec((tm, d), row)] if len(xs) == 1 else
                  [pl.BlockSpec((tm, d), p_row), pl.BlockSpec((tm, d), s_row)]) + pair_specs + [
            fixed(w_merge), fixed(w_bp), fixed(w_bg), fixed(w_bn), fixed(w_out), fixed(ln_g), fixed(ln_b),
            fixed(w_router_t), fixed(b_router)],
        out_specs=[pl.BlockSpec((tm, d), row), pl.BlockSpec((N_EXPERTS, tm), lambda i: (0, i))],
        out_shape=[jax.ShapeDtypeStruct((n, d), F32), jax.ShapeDtypeStruct((N_EXPERTS, n), F32)],
        compiler_params=_cparams("parallel"),
        name="merge_ln_route",
    )(*xs, *pair_args, w_merge, w_bp, w_bg, w_bn, w_out, ln_g, ln_b, w_router_t, b_router)


def _moe_kernel(x_ref, c_ref, wg_ref, wu_ref, wd_ref, lg_ref, lb_ref, y_ref, xb_s, acc_s, *, alpha):
    e = pl.program_id(1)

    @pl.when(e == 0)
    def _():
        xb_s[...] = x_ref[...].astype(BF16)
        acc_s[...] = jnp.zeros(acc_s.shape, acc_s.dtype)

    xb = xb_s[...]
    hg = _dot(xb, wg_ref[...])
    hu = _dot(xb, wu_ref[...])
    comb = c_ref[...]
    lane = lax.broadcasted_iota(jnp.int32, comb.shape, 1)
    ce = jnp.sum(jnp.where(lane == e, comb, 0.0), -1, keepdims=True)
    act = (hg * jax.nn.sigmoid(hg)) * hu * ce
    acc_s[...] += _dot(act.astype(BF16), wd_ref[...])

    @pl.when(e == pl.num_programs(1) - 1)
    def _():
        y_ref[...] = _layer_norm(alpha * x_ref[...] + acc_s[...], lg_ref[...], lb_ref[...])


def _moe(x, comb, w_gate, w_up, w_down, layer, ln_g, ln_b, alpha):
    n, d = x.shape
    tm = _tile(n, 1024)
    ne, _, de = w_gate.shape[1:]
    return pl.pallas_call(
        functools.partial(_moe_kernel, alpha=alpha),
        grid=(n // tm, ne),
        in_specs=[pl.BlockSpec((tm, d), lambda i, e: (i, 0)),
                  pl.BlockSpec((tm, ne), lambda i, e: (i, 0)),
                  pl.BlockSpec((None, None, d, de), lambda i, e: (layer, e, 0, 0)),
                  pl.BlockSpec((None, None, d, de), lambda i, e: (layer, e, 0, 0)),
                  pl.BlockSpec((None, None, de, d), lambda i, e: (layer, e, 0, 0)),
                  pl.BlockSpec(ln_g.shape, lambda i, e: (0, 0)),
                  pl.BlockSpec(ln_b.shape, lambda i, e: (0, 0))],
        out_specs=pl.BlockSpec((tm, d), lambda i, e: (i, 0)),
        out_shape=jax.ShapeDtypeStruct((n, d), F32),
        scratch_shapes=[pltpu.VMEM((tm, d), BF16), pltpu.VMEM((tm, d), F32)],
        compiler_params=_cparams("parallel", "arbitrary"),
        name="moe_ln",
    )(x, comb, w_gate, w_up, w_down, ln_g, ln_b)


def _rope_tables(pos):
    half = ROT_DIM // 2
    inv = 1.0 / (ROPE_THETA ** (jnp.arange(half, dtype=F32) / half))
    ang = pos.astype(F32)[:, None] * inv[None, :]
    cos, sin = jnp.cos(ang), jnp.sin(ang)
    n = pos.shape[0]
    one = jnp.ones((n, HEAD_DIM - ROT_DIM), F32)
    zero = jnp.zeros((n, HEAD_DIM - ROT_DIM), F32)
    zh = jnp.zeros((n, half), F32)
    rep = LANES // HEAD_DIM
    cos_t = jnp.tile(jnp.concatenate([cos, cos, one], 1), (1, rep))
    sm_t = jnp.tile(jnp.concatenate([-sin, zh, zero], 1), (1, rep))
    sp_t = jnp.tile(jnp.concatenate([zh, sin, zero], 1), (1, rep))
    return cos_t, sm_t, sp_t


def _overlap(nc, nsb, rows):
    i = np.arange(nc)[:, None]
    j = np.arange(nsb)[None, :]
    lo = np.maximum(i * CMP_STRIDE, j * SEL_LEN)
    hi = np.minimum(i * CMP_STRIDE + CMP_LEN, (j + 1) * SEL_LEN)
    ovl = np.zeros((rows, LANES), np.float32)
    ovl[:nc, :nsb] = np.maximum(hi - lo, 0) // CMP_STRIDE
    return jnp.asarray(ovl, BF16)


def _block_diag(blocks):
    n = len(blocks)
    rows = []
    for i, b in enumerate(blocks):
        rows.append(jnp.concatenate([b if j == i else jnp.zeros((b.shape[0], blocks[j].shape[1]), b.dtype)
                                     for j in range(n)], axis=1))
    return jnp.concatenate(rows, axis=0)


def _cmp_weights(w1, pe):
    w_out, pe_out = [], []
    for kv in range(2):
        w_ab, pe_ab = [], []
        for ab in range(2):
            blocks, pes = [], []
            for r in range(CMP_STRIDE):
                rr = ab * CMP_STRIDE + r
                w_r = w1[kv, rr * HEAD_DIM:(rr + 1) * HEAD_DIM, :]
                blocks.append(_block_diag([w_r] * N_KV_HEADS))
                pes.append(jnp.tile(pe[kv, rr], N_KV_HEADS))
            w_ab.append(jnp.concatenate(blocks, axis=0))
            pe_ab.append(jnp.concatenate(pes)[None, :])
        w_out.append(jnp.stack(w_ab))
        pe_out.append(jnp.stack(pe_ab))
    return jnp.stack(w_out).astype(BF16), jnp.stack(pe_out).astype(F32)


def _chunked_kt(k, nck):
    b, t, h, d = k.shape
    return k.transpose(0, 2, 3, 1).reshape(b, h, d, nck, t // nck).transpose(0, 1, 3, 2, 4).astype(BF16)


def _chunked_v(v, nck):
    b, t, h, d = v.shape
    v = v.transpose(0, 2, 1, 3).reshape(b, h, nck, t // nck, d).astype(BF16)
    return jnp.concatenate([v, jnp.ones_like(v)], axis=-1)


def kernel(x_prompt, x_sample, cache_cmp_kv, cache_sel_kv, state_win_kv, state_pool, page_table, w_in, pool_w, pool_scale, gmlp_ln_g, gmlp_ln_b, gmlp_ws, gmlp_bs, cmp_pe, cmp_w1, cmp_b1, cmp_w2, w_branch_pool, w_branch_gmlp, w_branch_nsa, w_out, ln1_g, ln1_b, ln2_g, ln2_b, w_router, b_router, w_gate, w_up, w_down):
    batch, seq, d = x_prompt.shape
    nb, tq, _ = x_sample.shape
    depth = w_in.shape[0]
    n_pool, page = cache_cmp_kv.shape[1:3]
    npages = page_table.shape[1]
    past = npages * page
    keep = state_win_kv.shape[2]
    pw = pool_scale.shape[-1]
    gw = gmlp_ln_g.shape[-1]
    kvw = N_KV_HEADS * HEAD_DIM
    n_q = N_HEADS * HEAD_DIM
    n_kv = 6 * kvw
    n_auv = pw + 2 * gw
    n_gate = 3 * N_HEADS
    alpha = (2 * depth) ** 0.25
    np_rows, ns_rows = batch * seq, nb * tq
    assert pw == gw and seq % KEY_CHUNK == 0 and seq % Q_BLOCK == 0 and tq <= CHUNK and tq % 8 == 0
    assert keep == WINDOW and keep % page == 0 and page % CMP_STRIDE == 0 and np_rows % tq == 0
    nc_p = (seq - CMP_LEN) // CMP_STRIDE + 1
    nc_s = (past + tq - CMP_LEN) // CMP_STRIDE + 1
    assert nc_p < seq // CMP_STRIDE and nc_s < past // CMP_STRIDE
    nsb_p = -(-seq // SEL_LEN)
    nsb_s = -(-(past + tq) // SEL_LEN)
    assert nsb_p <= SEL_ROWS and nsb_s <= LANES

    w_main = w_in[:, :, :n_auv + n_q + n_kv].astype(BF16)
    w_g = jnp.pad(w_in[:, :, n_auv + n_q + n_kv:n_auv + n_q + n_kv + n_gate], ((0, 0), (0, 0), (0, LANES - n_gate))).astype(BF16)
    w_merge = w_in[:, :, n_auv + n_q + n_kv + n_gate:].astype(BF16)
    w_bp, w_bg, w_bn, w_o = (w.astype(BF16) for w in (w_branch_pool, w_branch_gmlp, w_branch_nsa, w_out))
    wg_b, wu_b, wd_b = w_gate.astype(BF16), w_up.astype(BF16), w_down.astype(BF16)
    w_router_t = w_router.T.astype(F32)
    b_router_c = b_router.astype(F32)[:, None]
    grp_w = gw // GMLP_GROUPS

    pos_all = jnp.concatenate([jnp.tile(jnp.arange(seq), batch), jnp.tile(past + jnp.arange(tq), nb)])
    cos_t, sm_t, sp_t = _rope_tables(pos_all)
    ovl_p = _overlap(nc_p, nsb_p, seq // CMP_STRIDE)
    ovl_s = _overlap(nc_s, nsb_s, past // CMP_STRIDE)
    n_keys_s = (npages + 1) * page
    expand_s = jnp.asarray(np.arange(LANES)[:, None] == (np.arange(n_keys_s)[None, :] // SEL_LEN), BF16)
    expand_p = np.where(np.arange(SEL_ROWS)[:, None] == (np.arange(seq)[None, :] // SEL_LEN), NEG_INF, 0.0)
    expand_p = jnp.asarray(expand_p.reshape(SEL_ROWS, seq // KEY_CHUNK, KEY_CHUNK).transpose(1, 0, 2), BF16)

    xs = (x_prompt.reshape(np_rows, d), x_sample.reshape(ns_rows, d))
    rows_last = (0, 1, 3, 4, 5, 2)
    cache_cmp_t = cache_cmp_kv.transpose(rows_last)
    cache_sel_t = cache_sel_kv.transpose(rows_last)
    state_win_t = state_win_kv.transpose(rows_last)
    nck = seq // KEY_CHUNK

    st_p = [[] for _ in range(4)]
    st_s = [[] for _ in range(5)]
    for l in range(depth):
        z_auv, z_q, z_kv, z_g = _proj(xs, w_main[l], w_g[l], cos_t, sm_t, sp_t, n_auv, n_q, n_kv)

        pool_bd = _block_diag([pool_w[l, g] for g in range(len(POOL_WINDOWS))]).astype(BF16)
        ps, lg, lb = pool_scale[l][None, :], gmlp_ln_g[l][None, :], gmlp_ln_b[l][None, :]
        cl = min(CHUNK, seq)
        bs_rows = jnp.repeat(gmlp_bs[l][:, :cl].T, grp_w, axis=1)
        op_p, og_p = _mix_prompt(z_auv, batch, seq, pool_bd, ps, lg, lb, gmlp_ws[l], bs_rows)
        a_t = z_auv[np_rows:].reshape(nb, tq, n_auv).transpose(1, 0, 2)
        st_t = state_pool[l].transpose(1, 0, 2)
        wrow = jnp.repeat(gmlp_ws[l][:, :tq, :tq].transpose(1, 2, 0), grp_w, axis=2).reshape(tq * tq, gw)
        bsrow = jnp.repeat(gmlp_bs[l][:, :tq].T, grp_w, axis=1)
        op_s, og_s, vn_s = _mix_sample(a_t, st_t, float(past), pool_bd, ps, lg, lb, wrow, bsrow)
        to_rows = lambda y: y.transpose(1, 0, 2).reshape(ns_rows, -1)
        o_pool = (op_p, to_rows(op_s))
        o_gmlp = (og_p, to_rows(og_s))

        w1_bd, pe_rows = _cmp_weights(cmp_w1[l], cmp_pe[l])
        b1 = cmp_b1[l][:, None, :]
        w2kt = cmp_w2[l, 0].T.astype(BF16)
        w2v = cmp_w2[l, 1].astype(BF16)
        h1_p = _cmp1(z_kv, lambda tr: 0, np_rows, w1_bd, pe_rows)
        kct_p, vc_p = _cmp2_prompt(h1_p.reshape(batch, seq // CMP_STRIDE, -1), b1, w2kt, w2v)
        h1_pool = _cmp1_pages(cache_cmp_t, l, w1_bd, pe_rows)
        kct_s, vc_s = _cmp2_sample(page_table, h1_pool.reshape(n_pool, page // CMP_STRIDE, -1), b1, w2kt, w2v)

        zkv_p = z_kv[:np_rows].reshape(batch, seq, 3, 2, N_KV_HEADS, HEAD_DIM)
        o_p = _nsa_prompt(z_q, z_g, kct_p, vc_p,
                          _chunked_kt(zkv_p[:, :, 1, 0], nck), _chunked_v(zkv_p[:, :, 1, 1], nck),
                          _chunked_kt(zkv_p[:, :, 2, 0], nck), _chunked_v(zkv_p[:, :, 2, 1], nck),
                          ovl_p, expand_p, batch, seq)
        o_s = _nsa_sample(page_table, z_q, z_g, z_kv, np_rows, kct_s, vc_s, ovl_s, expand_s, cache_sel_t, state_win_t,
                          l, tq, past, nsb_s)
        o_nsa = (o_p, o_s.astype(BF16))

        x1, comb_t = _merge(xs, o_pool, o_gmlp, o_nsa, w_merge[l], w_bp[l], w_bg[l], w_bn[l], w_o[l],
                            ln1_g[l][None, :], ln1_b[l][None, :], w_router_t, b_router_c, alpha)
        x_all = _moe(x1, comb_t.T, wg_b, wu_b, wd_b, l, ln2_g[l][None, :], ln2_b[l][None, :], alpha)
        xs = (x_all,)

        kv_shape = (2, N_KV_HEADS, HEAD_DIM)
        zkv_s = z_kv[np_rows:].reshape(nb, tq, 3, *kv_shape)
        a_p = z_auv[:np_rows, :pw].reshape(batch, seq, pw)
        a_s = z_auv[np_rows:, :pw].reshape(nb, tq, pw)
        zkv_p = zkv_p.reshape(batch, seq, 3, *kv_shape)
        st_p[0].append(zkv_p[:, :, 0])
        st_p[1].append(zkv_p[:, :, 1])
        st_p[2].append(zkv_p[:, seq - min(WINDOW, seq):, 2])
        st_p[3].append(a_p[:, seq - POOL_KEEP:])
        st_s[0].append(zkv_s[:, :, 0])
        st_s[1].append(zkv_s[:, :, 1])
        st_s[2].append(zkv_s[:, :, 2])
        st_s[3].append(a_s)
        st_s[4].append(vn_s.transpose(1, 0, 2))

    y_prompt = x_all[:np_rows].reshape(batch, seq, d)
    y_sample = x_all[np_rows:].reshape(nb, tq, d)
    new_win = jnp.concatenate([state_win_kv, jnp.stack(st_s[2])], axis=2)[:, :, -keep:]
    new_pool = jnp.concatenate([state_pool, jnp.stack(st_s[3])], axis=2)[:, :, -POOL_KEEP:]
    return (y_prompt, y_sample,
            jnp.stack(st_p[0]), jnp.stack(st_p[1]), jnp.stack(st_p[2]), jnp.stack(st_p[3]),
            jnp.stack(st_s[0]), jnp.stack(st_s[1]), new_win, new_pool, jnp.stack(st_s[4]))
```

```python
import functools
import math

import numpy as np
import jax
import jax.numpy as jnp
from jax import lax
from jax.experimental import pallas as pl
from jax.experimental.pallas import tpu as pltpu

F32 = jnp.float32
BF16 = jnp.bfloat16

POOL_WINDOWS = (2, 4, 8, 16)
POOL_KEEP = max(POOL_WINDOWS) - 1
GMLP_GROUPS = 4
CHUNK = 128
N_HEADS = 8
HEAD_DIM = 64
N_KV_HEADS = 2
GQA = N_HEADS // N_KV_HEADS
ROT_DIM = HEAD_DIM // 4
ROPE_THETA = 500000.0
CMP_LEN = 32
CMP_STRIDE = 16
CMP_HIDDEN = 2 * HEAD_DIM
SEL_LEN = 64
SEL_TOPN = 16
WINDOW = 512
Q_BLOCK = 256
N_EXPERTS = 16
EXPERTS_PER_GROUP = 4
LN_EPS = 1e-5
NEG_INF = -1e30
FORCE = 1e9

LANES = 128
KEY_CHUNK = 512
SEL_ROWS = LANES - HEAD_DIM
VMEM_LIMIT = 56 * 1024 * 1024


def _cparams(*sem):
    return pltpu.CompilerParams(dimension_semantics=sem, vmem_limit_bytes=VMEM_LIMIT)


def _tile(n, cap):
    t = cap
    while n % t:
        t //= 2
    return t


def _dot(a, b):
    return jnp.dot(a, b, preferred_element_type=F32)


def _dot_nt(a, b):
    return lax.dot_general(a, b, (((1,), (1,)), ((), ())), preferred_element_type=F32)


def _layer_norm(x, g, b):
    mu = jnp.mean(x, -1, keepdims=True)
    xc = x - mu
    var = jnp.mean(xc * xc, -1, keepdims=True)
    return xc * lax.rsqrt(var + LN_EPS) * g + b


def _masked_softmax(s, mask):
    sm = jnp.where(mask, s, NEG_INF)
    p = jnp.where(mask, jnp.exp(sm - jnp.max(sm, -1, keepdims=True)), 0.0)
    d = jnp.sum(p, -1, keepdims=True)
    return p / jnp.where(d > 0, d, 1.0)


def _chunked_attention(lhs, rhs_fn, v_ref, bias_fn, c_lo, c_hi, s_sc, mx_sc, acc_sc):
    rows, width = mx_sc.shape
    reps = rows // Q_BLOCK

    mx_sc[...] = jnp.full(mx_sc.shape, -jnp.inf, F32)

    @pl.loop(c_lo, c_hi)
    def _(c):
        s = _dot(lhs, rhs_fn(c)) + jnp.concatenate([bias_fn(c)] * reps, axis=0)
        s_sc[c] = s
        mx = mx_sc[...]
        for j in range(s.shape[1] // width):
            mx = jnp.maximum(mx, s[:, j * width:(j + 1) * width])
        mx_sc[...] = mx

    m = jnp.max(mx_sc[...], -1, keepdims=True)
    acc_sc[...] = jnp.zeros(acc_sc.shape, F32)

    @pl.loop(c_lo, c_hi)
    def _(c):
        p = jnp.exp(s_sc[c] - m).astype(BF16)
        acc_sc[...] += _dot(p, v_ref[c])

    acc = acc_sc[...]
    return acc / pltpu.roll(acc, acc.shape[1] // 2, 1)


def _topk_rows(score_t, k, n_valid):
    nq = score_t.shape[1]
    sub_rows = 8
    nv = -(-n_valid // sub_rows)
    tiles = [score_t[sub_rows * v:sub_rows * (v + 1)] for v in range(nv)]
    sub = lax.broadcasted_iota(jnp.int32, (sub_rows, nq), 0)
    ranks = [jnp.zeros((sub_rows, nq), F32) for _ in range(nv)]
    for i in range(n_valid):
        vi, si = divmod(i, sub_rows)
        r = jnp.broadcast_to(score_t[i:i + 1], (sub_rows, nq))
        for v in range(nv):
            if v < vi:
                ahead = r > tiles[v]
            elif v > vi:
                ahead = r >= tiles[v]
            else:
                ahead = (r > tiles[v]) | ((r >= tiles[v]) & (sub > si))
            ranks[v] = ranks[v] + jnp.where(ahead, 1.0, 0.0)
    sel = [jnp.where((ranks[v] < k) & (sub_rows * v + sub < n_valid), 1.0, 0.0) for v in range(nv)]
    pad = score_t.shape[0] - sub_rows * nv
    if pad:
        sel.append(jnp.zeros((pad, nq), F32))
    return jnp.concatenate(sel, axis=0)


def _importance(p_cmp, rows, ovl):
    psum = p_cmp[0:rows]
    for g in range(1, GQA):
        psum = psum + p_cmp[g * rows:(g + 1) * rows]
    hi = psum.astype(BF16)
    lo = (psum - hi.astype(F32)).astype(BF16)
    return _dot(hi, ovl) + _dot(lo, ovl)


def _select_blocks(imp, qpos_t, nsb):
    imp_t = imp.T
    blk = lax.broadcasted_iota(jnp.int32, imp_t.shape, 0)
    cur = jnp.right_shift(qpos_t, int(math.log2(SEL_LEN)))
    forced = (blk == 0) | (blk == cur) | (blk == cur - 1)
    score_t = jnp.where(forced, FORCE, jnp.where(blk <= cur, imp_t, -FORCE))
    return _topk_rows(score_t, min(SEL_TOPN, nsb), nsb).T


def _head_gate(gs, col):
    lane = lax.broadcasted_iota(jnp.int32, gs.shape, 1)
    return jnp.sum(jnp.where(lane == col, gs, 0.0), -1, keepdims=True)


def _proj_kernel(*refs, n_auv, n_q, n_kv, prompt_tiles):
    x_refs, (w_ref, wg_ref, cos_ref, sm_ref, sp_ref, auv_ref, q_ref, kv_ref, g_ref) = refs[:-9], refs[-9:]
    x = x_refs[0][...]
    if len(x_refs) == 2:
        x = jnp.where(pl.program_id(0) < prompt_tiles, x, x_refs[1][...])
    xb = x.astype(BF16)
    z = _dot(xb, w_ref[...])
    auv_ref[...] = z[:, :n_auv]
    cos, sm, sp = cos_ref[...], sm_ref[...], sp_ref[...]
    half = ROT_DIM // 2

    def rope(t):
        return t * cos + pltpu.roll(t, LANES - half, 1) * sm + pltpu.roll(t, half, 1) * sp

    for j in range(n_q // LANES):
        c0 = n_auv + j * LANES
        q_ref[:, j * LANES:(j + 1) * LANES] = rope(z[:, c0:c0 + LANES]) * (HEAD_DIM ** -0.5)
    for j in range(n_kv // LANES):
        c0 = n_auv + n_q + j * LANES
        t = z[:, c0:c0 + LANES]
        kv_ref[:, j * LANES:(j + 1) * LANES] = rope(t) if j % 2 == 0 else t
    g_ref[...] = _dot(xb, wg_ref[...])


def _proj(xs, w_main, w_gate, cos, sm, sp, n_auv, n_q, n_kv):
    d = xs[0].shape[1]
    n = sum(x.shape[0] for x in xs)
    tm = _tile(math.gcd(*[x.shape[0] for x in xs]), 512)
    p_tiles = xs[0].shape[0] // tm
    row = lambda i: (i, 0)
    fixed = lambda i: (0, 0)
    if len(xs) == 1:
        x_specs = [pl.BlockSpec((tm, d), row)]
    else:
        x_specs = [pl.BlockSpec((tm, d), lambda i: (jnp.minimum(i, p_tiles - 1), 0)),
                   pl.BlockSpec((tm, d), lambda i: (jnp.maximum(i - p_tiles, 0), 0))]
    return pl.pallas_call(
        functools.partial(_proj_kernel, n_auv=n_auv, n_q=n_q, n_kv=n_kv, prompt_tiles=p_tiles),
        grid=(n // tm,),
        in_specs=x_specs + [
                  pl.BlockSpec(w_main.shape, fixed),
                  pl.BlockSpec(w_gate.shape, fixed),
                  pl.BlockSpec((tm, LANES), row), pl.BlockSpec((tm, LANES), row), pl.BlockSpec((tm, LANES), row)],
        out_specs=[pl.BlockSpec((tm, n_auv), row), pl.BlockSpec((tm, n_q), row),
                   pl.BlockSpec((tm, n_kv), row), pl.BlockSpec((tm, LANES), row)],
        out_shape=[jax.ShapeDtypeStruct((n, n_auv), F32), jax.ShapeDtypeStruct((n, n_q), F32),
                   jax.ShapeDtypeStruct((n, n_kv), F32), jax.ShapeDtypeStruct((n, LANES), F32)],
        compiler_params=_cparams("parallel"),
        name="in_proj",
    )(*xs, w_main, w_gate, cos, sm, sp)


def _pool_select(grp, vals):
    out = vals[-1]
    for gi in range(len(vals) - 2, -1, -1):
        out = jnp.where(grp == gi, vals[gi], out)
    return out


def _mix_prompt_kernel(auv_ref, halo_ref, pw_ref, ps_ref, lng_ref, lnb_ref, ws_ref, bsr_ref, op_ref, og_ref, *, tm, pw):
    i = pl.program_id(1)
    halo_rows = halo_ref.shape[0]
    a = auv_ref[:, 0:pw]
    u = auv_ref[:, pw:2 * pw]
    v = auv_ref[:, 2 * pw:3 * pw]
    halo = jnp.where(i == 0, 0.0, halo_ref[...])
    ext = jnp.concatenate([halo, a], axis=0)
    sums, cur, width = [], ext, 1
    for win in POOL_WINDOWS:
        while width < win:
            cur = cur + pltpu.roll(cur, width, 0)
            width *= 2
        sums.append(cur[halo_rows:])
    lane = lax.broadcasted_iota(jnp.int32, (tm, pw), 1)
    grp = lane // (pw // len(POOL_WINDOWS))
    pos = (i * tm + lax.broadcasted_iota(jnp.int32, (tm, pw), 0)).astype(F32)
    total = _pool_select(grp, sums)
    win = _pool_select(grp, [jnp.full((tm, pw), float(w), F32) for w in POOL_WINDOWS])
    d = total / jnp.minimum(win, pos + 1.0) - a
    op_ref[...] = (_dot(d.astype(BF16), pw_ref[...]) * ps_ref[...]).astype(op_ref.dtype)

    vn = _layer_norm(v, lng_ref[...], lnb_ref[...]).astype(BF16)
    r = lax.broadcasted_iota(jnp.int32, (CHUNK, CHUNK), 0)
    c = lax.broadcasted_iota(jnp.int32, (CHUNK, CHUNK), 1)
    wms = [jnp.where(c <= r, ws_ref[g], 0.0).astype(BF16) for g in range(GMLP_GROUPS)]
    grp_c = lax.broadcasted_iota(jnp.int32, (CHUNK, pw), 1) // (pw // GMLP_GROUPS)
    for j in range(tm // CHUNK):
        rows = slice(j * CHUNK, (j + 1) * CHUNK)
        vc = vn[rows]
        mix = _pool_select(grp_c, [_dot(wms[g], vc) for g in range(GMLP_GROUPS)])
        og_ref[rows, :] = (u[rows] * (mix + bsr_ref[...])).astype(og_ref.dtype)


def _mix_prompt(z_auv, batch, seq, pool_bd, pool_scale, ln_g, ln_b, ws, bs_rows):
    pw = pool_scale.shape[-1]
    tm = _tile(seq, 512)
    nt = seq // tm
    halo = 2 * 8
    hb = tm // halo
    fixed2 = lambda b, i: (0, 0)
    return pl.pallas_call(
        functools.partial(_mix_prompt_kernel, tm=tm, pw=pw),
        grid=(batch, nt),
        in_specs=[pl.BlockSpec((tm, 3 * pw), lambda b, i: (b * nt + i, 0)),
                  pl.BlockSpec((halo, pw), lambda b, i: (jnp.maximum((b * nt + i) * hb - 1, 0), 0)),
                  pl.BlockSpec((pw, pw), fixed2), pl.BlockSpec((1, pw), fixed2),
                  pl.BlockSpec((1, pw), fixed2), pl.BlockSpec((1, pw), fixed2),
                  pl.BlockSpec(ws.shape, lambda b, i: (0, 0, 0)),
                  pl.BlockSpec((CHUNK, pw), fixed2)],
        out_specs=[pl.BlockSpec((tm, pw), lambda b, i: (b * nt + i, 0)),
                   pl.BlockSpec((tm, pw), lambda b, i: (b * nt + i, 0))],
        out_shape=[jax.ShapeDtypeStruct((batch * seq, pw), BF16), jax.ShapeDtypeStruct((batch * seq, pw), BF16)],
        compiler_params=_cparams("parallel", "parallel"),
        name="mix_prompt",
    )(z_auv, z_auv, pool_bd, pool_scale, ln_g, ln_b, ws, bs_rows)


def _mix_sample_kernel(a_ref, st_ref, pw_ref, ps_ref, lng_ref, lnb_ref, wrow_ref, bsr_ref, op_ref, og_ref, vn_ref,
                       *, tq, keep, pos0, pw):
    nb = a_ref.shape[1]
    ext = [st_ref[i] for i in range(keep)] + [a_ref[t][:, 0:pw] for t in range(tq)]
    grp = lax.broadcasted_iota(jnp.int32, (nb, pw), 1) // (pw // len(POOL_WINDOWS))
    for t in range(tq):
        e = keep + t
        acc, k, sums = ext[e], 1, []
        for win in POOL_WINDOWS:
            while k < win:
                if e - k >= 0:
                    acc = acc + ext[e - k]
                k += 1
            sums.append(acc)
        total = _pool_select(grp, sums)
        cnt = _pool_select(grp, [jnp.full((nb, pw), min(float(w), pos0 + t + 1.0), F32) for w in POOL_WINDOWS])
        d = total / cnt - ext[e]
        op_ref[t] = (_dot(d.astype(BF16), pw_ref[...]) * ps_ref[...]).astype(op_ref.dtype)
    vns = []
    for t in range(tq):
        vn = _layer_norm(a_ref[t][:, 2 * pw:3 * pw], lng_ref[...], lnb_ref[...])
        vn_ref[t] = vn
        vns.append(vn)
    for t in range(tq):
        mix = bsr_ref[t:t + 1, :]
        for s in range(t + 1):
            mix = mix + wrow_ref[t * tq + s:t * tq + s + 1, :] * vns[s]
        og_ref[t] = (a_ref[t][:, pw:2 * pw] * mix).astype(og_ref.dtype)


def _mix_sample(a_t, st_t, pos0, pool_bd, pool_scale, ln_g, ln_b, wrow, bsrow):
    tq, nb, w3 = a_t.shape
    keep = st_t.shape[0]
    pw = w3 // 3
    full = lambda shape: pl.BlockSpec(shape, lambda i: (0,) * len(shape))
    return pl.pallas_call(
        functools.partial(_mix_sample_kernel, tq=tq, keep=keep, pos0=pos0, pw=pw),
        grid=(1,),
        in_specs=[full(a_t.shape), full(st_t.shape), full(pool_bd.shape), full(pool_scale.shape),
                  full(ln_g.shape), full(ln_b.shape), full(wrow.shape), full(bsrow.shape)],
        out_specs=[full((tq, nb, pw)), full((tq, nb, pw)), full((tq, nb, pw))],
        out_shape=[jax.ShapeDtypeStruct((tq, nb, pw), BF16), jax.ShapeDtypeStruct((tq, nb, pw), BF16),
                   jax.ShapeDtypeStruct((tq, nb, pw), F32)],
        compiler_params=_cparams("arbitrary"),
        name="mix_sample",
    )(a_t, st_t, pool_bd, pool_scale, ln_g, ln_b, wrow, bsrow)


def _cmp1_kernel(x_ref, w_ref, pe_ref, h_ref, *, g):
    hw = N_KV_HEADS * CMP_HIDDEN
    xcat = jnp.concatenate([x_ref[pl.ds(r, g, stride=CMP_STRIDE), :] for r in range(CMP_STRIDE)], axis=1)
    for ab in range(2):
        lhs = (xcat + pe_ref[ab]).astype(BF16)
        h_ref[:, ab * hw:(ab + 1) * hw] = _dot(lhs, w_ref[ab])


def _cmp1(x2d, row_block0, n_rows, w1_bd, pe_rows):
    tr = _tile(n_rows, 4096)
    g = tr // CMP_STRIDE
    kvw = N_KV_HEADS * HEAD_DIM
    hcols = 2 * N_KV_HEADS * CMP_HIDDEN
    return pl.pallas_call(
        functools.partial(_cmp1_kernel, g=g),
        grid=(n_rows // tr, 2),
        in_specs=[pl.BlockSpec((tr, kvw), lambda i, kv: (row_block0(tr) + i, kv)),
                  pl.BlockSpec((None,) + w1_bd.shape[1:], lambda i, kv: (kv, 0, 0, 0)),
                  pl.BlockSpec((None,) + pe_rows.shape[1:], lambda i, kv: (kv, 0, 0, 0))],
        out_specs=pl.BlockSpec((g, hcols), lambda i, kv: (i, kv)),
        out_shape=jax.ShapeDtypeStruct((n_rows // CMP_STRIDE, 2 * hcols), F32),
        compiler_params=_cparams("parallel", "parallel"),
        name="cmp_stage1",
    )(x2d, w1_bd, pe_rows)


def _cmp1_pages_kernel(xt_ref, w_ref, pe_ref, h_ref, *x_scs, g):
    n_pages, n_h, hd, page = xt_ref.shape
    parts = len(x_scs)
    pp, gp = n_pages // parts, g // parts
    hw = N_KV_HEADS * CMP_HIDDEN
    for k, x_sc in enumerate(x_scs):
        for p in range(pp):
            x_sc[p * page:(p + 1) * page, :] = xt_ref[k * pp + p].reshape(n_h * hd, page).T
        xcat = jnp.concatenate([x_sc[pl.ds(r, gp, stride=CMP_STRIDE), :] for r in range(CMP_STRIDE)], axis=1)
        for ab in range(2):
            lhs = (xcat + pe_ref[ab]).astype(BF16)
            h_ref[k * gp:(k + 1) * gp, ab * hw:(ab + 1) * hw] = _dot(lhs, w_ref[ab])


def _cmp1_pages(cache_t, layer, w1_bd, pe_rows):
    _, n_pool, _, n_h, hd, page = cache_t.shape
    pp = _tile(n_pool, 8192 // page)
    g = pp * page // CMP_STRIDE
    parts = 2 if pp % 2 == 0 else 1
    hcols = 2 * N_KV_HEADS * CMP_HIDDEN
    return pl.pallas_call(
        functools.partial(_cmp1_pages_kernel, g=g),
        grid=(n_pool // pp, 2),
        in_specs=[pl.BlockSpec((None, pp, None, n_h, hd, page), lambda i, kv: (layer, i, kv, 0, 0, 0)),
                  pl.BlockSpec((None,) + w1_bd.shape[1:], lambda i, kv: (kv, 0, 0, 0)),
                  pl.BlockSpec((None,) + pe_rows.shape[1:], lambda i, kv: (kv, 0, 0, 0))],
        out_specs=pl.BlockSpec((g, hcols), lambda i, kv: (i, kv)),
        out_shape=jax.ShapeDtypeStruct((n_pool * page // CMP_STRIDE, 2 * hcols), F32),
        scratch_shapes=[pltpu.VMEM((pp // parts * page, n_h * hd), F32)] * parts,
        compiler_params=_cparams("parallel", "parallel"),
        name="cmp_stage1_pages",
    )(cache_t, w1_bd, pe_rows)


def _gelu_tanh(x):
    return 0.5 * x * (1.0 + jnp.tanh(math.sqrt(2.0 / math.pi) * (x + 0.044715 * (x * x * x))))


def _cmp2_body(h_refs, b1_ref, w2kt_ref, w2v_ref, kct_ref, vc_ref):
    hall = jnp.concatenate([r[...] for r in h_refs], axis=0) if len(h_refs) > 1 else h_refs[0][...]
    g = hall.shape[0]
    hw = N_KV_HEADS * CMP_HIDDEN
    for kv in range(2):
        for h in range(N_KV_HEADS):
            ca = (kv * 2) * hw + h * CMP_HIDDEN
            cb = (kv * 2 + 1) * hw + h * CMP_HIDDEN
            hid = hall[:, ca:ca + CMP_HIDDEN] + pltpu.roll(hall[:, cb:cb + CMP_HIDDEN], g - 1, 0) + b1_ref[kv]
            hid = _gelu_tanh(hid).astype(BF16)
            if kv == 0:
                kct_ref[h] = _dot_nt(w2kt_ref[...], hid).astype(kct_ref.dtype)
            else:
                vc_ref[h] = _dot(hid, w2v_ref[...]).astype(vc_ref.dtype)


def _cmp2_prompt_kernel(h_ref, b1_ref, w2kt_ref, w2v_ref, kct_ref, vc_ref):
    _cmp2_body([h_ref], b1_ref, w2kt_ref, w2v_ref, kct_ref, vc_ref)


def _cmp2_sample_kernel(pt_ref, *refs, npages, nu):
    b1_ref, w2kt_ref, w2v_ref, kct_ref, vc_ref = refs[nu * npages:]
    for u in range(nu):
        _cmp2_body(list(refs[u * npages:(u + 1) * npages]), b1_ref, w2kt_ref, w2v_ref, kct_ref.at[u], vc_ref.at[u])


def _cmp2_out(n, g):
    specs = [pl.BlockSpec((None, N_KV_HEADS, HEAD_DIM, g), lambda b, *_: (b, 0, 0, 0)),
             pl.BlockSpec((None, N_KV_HEADS, g, HEAD_DIM), lambda b, *_: (b, 0, 0, 0))]
    shapes = [jax.ShapeDtypeStruct((n, N_KV_HEADS, HEAD_DIM, g), BF16),
              jax.ShapeDtypeStruct((n, N_KV_HEADS, g, HEAD_DIM), BF16)]
    return specs, shapes


def _cmp2_prompt(h1, b1, w2kt, w2v):
    n, g, hc = h1.shape
    out_specs, out_shape = _cmp2_out(n, g)
    return pl.pallas_call(
        _cmp2_prompt_kernel,
        grid=(n,),
        in_specs=[pl.BlockSpec((None, g, hc), lambda b: (b, 0, 0)),
                  pl.BlockSpec(b1.shape, lambda b: (0, 0, 0)),
                  pl.BlockSpec(w2kt.shape, lambda b: (0, 0)),
                  pl.BlockSpec(w2v.shape, lambda b: (0, 0))],
        out_specs=out_specs, out_shape=out_shape,
        compiler_params=_cparams("parallel"),
        name="cmp_stage2_prompt",
    )(h1, b1, w2kt, w2v)


def _cmp2_sample(page_table, h1_pool, b1, w2kt, w2v):
    n, npages = page_table.shape
    _, gp, hc = h1_pool.shape
    g = npages * gp
    nu = max(u for u in (1, 2, 4) if n % u == 0)
    page_specs = [pl.BlockSpec((None, gp, hc), functools.partial(lambda b, pt, u, j: (pt[b * nu + u, j], 0, 0), u=u, j=j))
                  for u in range(nu) for j in range(npages)]
    return pl.pallas_call(
        functools.partial(_cmp2_sample_kernel, npages=npages, nu=nu),
        grid_spec=pltpu.PrefetchScalarGridSpec(
            num_scalar_prefetch=1, grid=(n // nu,),
            in_specs=page_specs + [pl.BlockSpec(b1.shape, lambda b, pt: (0, 0, 0)),
                                   pl.BlockSpec(w2kt.shape, lambda b, pt: (0, 0)),
                                   pl.BlockSpec(w2v.shape, lambda b, pt: (0, 0))],
            out_specs=[pl.BlockSpec((nu, N_KV_HEADS, HEAD_DIM, g), lambda b, pt: (b, 0, 0, 0)),
                       pl.BlockSpec((nu, N_KV_HEADS, g, HEAD_DIM), lambda b, pt: (b, 0, 0, 0))]),
        out_shape=_cmp2_out(n, g)[1],
        compiler_params=_cparams("parallel"),
        name="cmp_stage2_sample",
    )(page_table, *([h1_pool] * (nu * npages)), b1, w2kt, w2v)


def _nsa_prompt_kernel(q_ref, g_ref, kct_ref, vc_ref, kts_ref, vs_ref, ktw_ref, vw_ref, ovl_ref, exp_ref, o_ref,
                       s_sc, mx_sc, acc_sc, *, nsb):
    h = pl.program_id(1)
    qb = pl.program_id(2)
    tq = Q_BLOCK
    rows = GQA * tq
    t0 = qb * tq
    qf = q_ref[...]
    q = jnp.concatenate([qf[:, g * HEAD_DIM:(g + 1) * HEAD_DIM] for g in range(GQA)], axis=0).astype(BF16)
    qpos_r = t0 + (lax.broadcasted_iota(jnp.int32, (rows, 1), 0) & (tq - 1))

    ncp = kct_ref.shape[-1]
    s = _dot(q, kct_ref[...])
    cend = lax.broadcasted_iota(jnp.int32, (1, ncp), 1) * CMP_STRIDE + (CMP_LEN - 1)
    p_cmp = _masked_softmax(s, cend <= qpos_r)
    o_cmp = _dot(p_cmp.astype(BF16), vc_ref[...])

    imp = _importance(p_cmp, tq, ovl_ref[...])
    qpos_t = t0 + lax.broadcasted_iota(jnp.int32, (imp.shape[1], imp.shape[0]), 1)
    selm = _select_blocks(imp, qpos_t, nsb)
    qpos_c = t0 + lax.broadcasted_iota(jnp.int32, (tq, 1), 0)
    kcol = lax.broadcasted_iota(jnp.int32, (1, KEY_CHUNK), 1)

    hidden = jnp.concatenate([1.0 - selm[:, :SEL_ROWS]] * GQA, axis=0).astype(BF16)
    q_sel = jnp.concatenate([q, hidden], axis=1)

    def sel_rhs(c):
        return jnp.concatenate([kts_ref[c], exp_ref[c]], axis=0)

    def sel_bias(c):
        return jnp.where(c * KEY_CHUNK + kcol <= qpos_c, 0.0, NEG_INF)

    def win_bias(c):
        kpos = c * KEY_CHUNK + kcol
        return jnp.where((kpos <= qpos_c) & (kpos > qpos_c - WINDOW), 0.0, NEG_INF)

    c_hi = (t0 + tq - 1) // KEY_CHUNK + 1
    c_lo = jnp.maximum(t0 - WINDOW + 1, 0) // KEY_CHUNK
    o_sel = _chunked_attention(q_sel, sel_rhs, vs_ref, sel_bias, 0, c_hi, s_sc, mx_sc, acc_sc)
    o_win = _chunked_attention(q, lambda c: ktw_ref[c], vw_ref, win_bias, c_lo, c_hi, s_sc, mx_sc, acc_sc)

    gs = jax.nn.sigmoid(g_ref[...])
    outs = []
    for g in range(GQA):
        r = slice(g * tq, (g + 1) * tq)
        col = (h * GQA + g) * 3
        outs.append(_head_gate(gs, col) * o_cmp[r] + _head_gate(gs, col + 1) * o_sel[r, :HEAD_DIM]
                    + _head_gate(gs, col + 2) * o_win[r, :HEAD_DIM])
    o_ref[...] = jnp.concatenate(outs, axis=1).astype(o_ref.dtype)


def _nsa_prompt(z_q, z_g, kct, vc, kts, vs, ktw, vw, ovl, expand, batch, seq):
    nqb = seq // Q_BLOCK
    hq = GQA * HEAD_DIM
    nsb = -(-seq // SEL_LEN)
    per_head = lambda shape: pl.BlockSpec((None, None) + shape, lambda b, h, i: (b, h) + (0,) * len(shape))
    return pl.pallas_call(
        functools.partial(_nsa_prompt_kernel, nsb=nsb),
        grid=(batch, N_KV_HEADS, nqb),
        in_specs=[pl.BlockSpec((Q_BLOCK, hq), lambda b, h, i: (b * nqb + i, h)),
                  pl.BlockSpec((Q_BLOCK, LANES), lambda b, h, i: (b * nqb + i, 0)),
                  per_head(kct.shape[2:]), per_head(vc.shape[2:]),
                  per_head(kts.shape[2:]), per_head(vs.shape[2:]),
                  per_head(ktw.shape[2:]), per_head(vw.shape[2:]),
                  pl.BlockSpec(ovl.shape, lambda b, h, i: (0, 0)),
                  pl.BlockSpec(expand.shape, lambda b, h, i: (0, 0, 0))],
        out_specs=pl.BlockSpec((Q_BLOCK, hq), lambda b, h, i: (b * nqb + i, h)),
        out_shape=jax.ShapeDtypeStruct((batch * seq, N_KV_HEADS * hq), BF16),
        scratch_shapes=[pltpu.VMEM((seq // KEY_CHUNK, GQA * Q_BLOCK, KEY_CHUNK), F32),
                        pltpu.VMEM((GQA * Q_BLOCK, LANES), F32), pltpu.VMEM((GQA * Q_BLOCK, vs.shape[-1]), F32)],
        compiler_params=_cparams("parallel", "parallel", "parallel"),
        name="nsa_prompt",
    )(z_q, z_g, kct, vc, kts, vs, ktw, vw, ovl, expand)


def _nsa_sample_kernel(pt_ref, q_ref, g_ref, kvn_ref, kct_ref, vc_ref, ovl_ref, exp_ref, win_ref, *rest,
                       npages, tq, pos0, nsb, nu):
    page_refs = rest[:nu * npages]
    o_ref = rest[nu * npages]
    newpg = rest[nu * npages + 1]
    page = newpg.shape[1]
    kvw = N_KV_HEADS * HEAD_DIM
    newpg[...] = jnp.zeros(newpg.shape, newpg.dtype)
    for u in range(nu):
        newpg[u, 0:tq, :] = kvn_ref[u * tq:(u + 1) * tq, :]
    rows = GQA * tq
    qpos_r = pos0 + lax.rem(lax.broadcasted_iota(jnp.int32, (rows, 1), 0), tq)
    keep = win_ref.shape[-1]
    units = [(u, h) for u in range(nu) for h in range(N_KV_HEADS)]
    qfs = [q_ref[u * tq:(u + 1) * tq, :] for u in range(nu)]
    gss = [jax.nn.sigmoid(g_ref[u * tq:(u + 1) * tq, :]) for u in range(nu)]
    qs = [jnp.concatenate([qfs[u][:, (h * GQA + g) * HEAD_DIM:(h * GQA + g + 1) * HEAD_DIM] for g in range(GQA)],
                          axis=0).astype(BF16) for u, h in units]

    ncp = kct_ref.shape[-1]
    cend = lax.broadcasted_iota(jnp.int32, (1, ncp), 1) * CMP_STRIDE + (CMP_LEN - 1)
    p_cmps = [_masked_softmax(_dot(qs[i], kct_ref[u, h]), cend <= qpos_r) for i, (u, h) in enumerate(units)]
    o_cmps = [_dot(p_cmps[i].astype(BF16), vc_ref[u, h]) for i, (u, h) in enumerate(units)]

    imps = [_importance(p, tq, ovl_ref[...]) for p in p_cmps]
    imp = jnp.concatenate(imps + [jnp.zeros((LANES - len(units) * tq, LANES), F32)], axis=0)
    qpos_t = pos0 + lax.rem(lax.broadcasted_iota(jnp.int32, (LANES, LANES), 1), tq)
    selm_all = _select_blocks(imp, qpos_t, nsb)

    nun = range(len(units))
    kcs = [h * HEAD_DIM for _, h in units]
    vcs = [kvw + h * HEAD_DIM for _, h in units]
    pages = [page_refs[u * npages:(u + 1) * npages] for u, _ in units]

    es = [_dot(jnp.concatenate([selm_all[i * tq:(i + 1) * tq]] * GQA, axis=0).astype(BF16), exp_ref[...]) for i in nun]
    k_new = [newpg[u, :, 2 * kvw + kcs[i]:2 * kvw + kcs[i] + HEAD_DIM].astype(BF16) for i, (u, _) in enumerate(units)]
    v_new = [newpg[u, :, 2 * kvw + vcs[i]:2 * kvw + vcs[i] + HEAD_DIM].astype(BF16) for i, (u, _) in enumerate(units)]
    ss = [jnp.concatenate([_dot(qs[i], pages[i][p][0, h].astype(BF16)) for p in range(npages)]
                          + [_dot_nt(qs[i], k_new[i])], axis=1) for i, (_, h) in enumerate(units)]
    kpos = lax.broadcasted_iota(jnp.int32, (1, ss[0].shape[1]), 1)
    ps = [_masked_softmax(ss[i], (es[i] > 0.5) & (kpos <= qpos_r)).astype(BF16) for i in nun]
    o_sels = []
    for i, (_, h) in enumerate(units):
        o_sel = _dot(ps[i][:, npages * page:], v_new[i])
        for j in range(npages):
            o_sel = o_sel + _dot_nt(ps[i][:, j * page:(j + 1) * page], pages[i][j][1, h].astype(BF16))
        o_sels.append(o_sel)

    k_new = [newpg[u, :, 4 * kvw + kcs[i]:4 * kvw + kcs[i] + HEAD_DIM].astype(BF16) for i, (u, _) in enumerate(units)]
    v_new = [newpg[u, :, 4 * kvw + vcs[i]:4 * kvw + vcs[i] + HEAD_DIM].astype(BF16) for i, (u, _) in enumerate(units)]
    ss = [jnp.concatenate([_dot(qs[i], win_ref[u, 0, h].astype(BF16)), _dot_nt(qs[i], k_new[i])], axis=1)
          for i, (u, h) in enumerate(units)]
    kpos = pos0 - keep + lax.broadcasted_iota(jnp.int32, (1, ss[0].shape[1]), 1)
    ps = [_masked_softmax(ss[i], (kpos <= qpos_r) & (kpos > qpos_r - WINDOW)).astype(BF16) for i in nun]
    o_wins = [_dot_nt(ps[i][:, :keep], win_ref[u, 1, h].astype(BF16)) + _dot(ps[i][:, keep:], v_new[i])
              for i, (u, h) in enumerate(units)]

    outs = [[] for _ in range(nu)]
    for i, (u, h) in enumerate(units):
        for g in range(GQA):
            r = slice(g * tq, (g + 1) * tq)
            col = (h * GQA + g) * 3
            outs[u].append(_head_gate(gss[u], col) * o_cmps[i][r] + _head_gate(gss[u], col + 1) * o_sels[i][r]
                           + _head_gate(gss[u], col + 2) * o_wins[i][r])
    for u in range(nu):
        o_ref[u * tq:(u + 1) * tq, :] = jnp.concatenate(outs[u], axis=1).astype(o_ref.dtype)


def _nsa_sample(page_table, z_q, z_g, z_kv, row0, kct, vc, ovl, expand, cache_sel, state_win, layer, tq, pos0, nsb):
    n, npages = page_table.shape
    page = cache_sel.shape[-1]
    nu = max(u for u in (1, 2, 4) if n % u == 0 and row0 % (u * tq) == 0)
    rows = nu * tq
    blk0 = row0 // rows
    hq = z_q.shape[1]
    page_specs = [pl.BlockSpec((None, None) + cache_sel.shape[2:],
                               functools.partial(lambda b, pt, u, j: (layer, pt[b * nu + u, j], 0, 0, 0, 0), u=u, j=j))
                  for u in range(nu) for j in range(npages)]
    return pl.pallas_call(
        functools.partial(_nsa_sample_kernel, npages=npages, tq=tq, pos0=pos0, nsb=nsb, nu=nu),
        grid_spec=pltpu.PrefetchScalarGridSpec(
            num_scalar_prefetch=1, grid=(n // nu,),
            in_specs=[pl.BlockSpec((rows, hq), lambda b, pt: (blk0 + b, 0)),
                      pl.BlockSpec((rows, LANES), lambda b, pt: (blk0 + b, 0)),
                      pl.BlockSpec((rows, z_kv.shape[1]), lambda b, pt: (blk0 + b, 0)),
                      pl.BlockSpec((nu,) + kct.shape[1:], lambda b, pt: (b, 0, 0, 0)),
                      pl.BlockSpec((nu,) + vc.shape[1:], lambda b, pt: (b, 0, 0, 0)),
                      pl.BlockSpec(ovl.shape, lambda b, pt: (0, 0)),
                      pl.BlockSpec(expand.shape, lambda b, pt: (0, 0)),
                      pl.BlockSpec((None, nu) + state_win.shape[2:], lambda b, pt: (layer, b, 0, 0, 0, 0))] + page_specs,
            out_specs=pl.BlockSpec((rows, hq), lambda b, pt: (b, 0)),
            scratch_shapes=[pltpu.VMEM((nu, page, z_kv.shape[1]), F32)]),
        out_shape=jax.ShapeDtypeStruct((n * tq, hq), F32),
        compiler_params=_cparams("parallel"),
        name="nsa_sample",
    )(page_table, z_q, z_g, z_kv, kct, vc, ovl, expand, state_win, *([cache_sel] * (nu * npages)))


def _route(x1, wrt, br, ct_ref):
    logits = lax.dot_general(wrt, x1, (((1,), (1,)), ((), ())), precision=lax.Precision.HIGHEST,
                             preferred_element_type=F32)
    aff = jax.nn.sigmoid(logits)
    score = aff + br
    sc = [score[e:e + 1, :] for e in range(N_EXPERTS)]
    af = [aff[e:e + 1, :] for e in range(N_EXPERTS)]
    n_grp = N_EXPERTS // EXPERTS_PER_GROUP
    gsum = []
    for k in range(n_grp):
        m = sc[k * EXPERTS_PER_GROUP:(k + 1) * EXPERTS_PER_GROUP]
        best = None
        for i in range(EXPERTS_PER_GROUP):
            for j in range(i + 1, EXPERTS_PER_GROUP):
                pair = m[i] + m[j]
                best = pair if best is None else jnp.maximum(best, pair)
        gsum.append(best)
    is_best, taken = [], None
    for k in range(n_grp):
        ok = None
        for j in range(k + 1, n_grp):
            c = gsum[k] >= gsum[j]
            ok = c if ok is None else ok & c
        if ok is None:
            ok = jnp.ones_like(gsum[k], dtype=jnp.bool_)
        if taken is not None:
            ok = ok & jnp.logical_not(taken)
        is_best.append(ok)
        taken = ok if taken is None else taken | ok
    sel = []
    for e in range(N_EXPERTS):
        k = e // EXPERTS_PER_GROUP
        rank = jnp.zeros_like(sc[e])
        for e2 in range(k * EXPERTS_PER_GROUP, (k + 1) * EXPERTS_PER_GROUP):
            if e2 == e:
                continue
            ahead = (sc[e2] >= sc[e]) if e2 < e else (sc[e2] > sc[e])
            rank = rank + ahead.astype(F32)
        sel.append(is_best[k] & (rank < 2.0))
    denom = jnp.zeros_like(sc[0])
    for e in range(N_EXPERTS):
        denom = denom + jnp.where(sel[e], af[e], 0.0)
    for e in range(N_EXPERTS):
        ct_ref[e:e + 1, :] = jnp.where(sel[e], af[e] / denom, 0.0)


def _merge_kernel(*refs, alpha, prompt_tiles, n_x):
    x_refs = refs[:n_x]
    (opp_ref, ops_ref, ogp_ref, ogs_ref, onp_ref, ons_ref, wm_ref, wp_ref, wgm_ref, wn_ref, wo_ref,
     lg_ref, lb_ref, wrt_ref, br_ref, x1_ref, ct_ref) = refs[n_x:]
    from_prompt = pl.program_id(0) < prompt_tiles
    x = x_refs[0][...]
    if n_x == 2:
        x = jnp.where(from_prompt, x, x_refs[1][...])
    xb = x.astype(BF16)
    d = x.shape[1]
    merged = None
    for b, (p_r, s_r, w_r) in enumerate(((opp_ref, ops_ref, wp_ref), (ogp_ref, ogs_ref, wgm_ref), (onp_ref, ons_ref, wn_ref))):
        gate = jax.nn.sigmoid(_dot(xb, wm_ref[:, b * d:(b + 1) * d]))
        term = gate * _dot(jnp.where(from_prompt, p_r[...], s_r[...]), w_r[...])
        merged = term if merged is None else merged + term
    out = _dot(merged.astype(BF16), wo_ref[...])
    x1 = _layer_norm(alpha * x + out, lg_ref[...], lb_ref[...])
    x1_ref[...] = x1
    _route(x1, wrt_ref[...], br_ref[...], ct_ref)


def _merge(xs, o_pool, o_gmlp, o_nsa, w_merge, w_bp, w_bg, w_bn, w_out, ln_g, ln_b, w_router_t, b_router, alpha):
    d = xs[0].shape[1]
    n = sum(x.shape[0] for x in xs)
    n_p, n_s = o_pool[0].shape[0], o_pool[1].shape[0]
    tm = _tile(math.gcd(n_p, n_s), 512)
    p_tiles = n_p // tm
    row = lambda i: (i, 0)
    p_row = lambda i: (jnp.minimum(i, p_tiles - 1), 0)
    s_row = lambda i: (jnp.maximum(i - p_tiles, 0), 0)
    fixed = lambda a: pl.BlockSpec(a.shape, lambda i: (0, 0))
    pair_specs, pair_args = [], []
    for o_p, o_s in (o_pool, o_gmlp, o_nsa):
        pair_specs += [pl.BlockSpec((tm, o_p.shape[1]), p_row), pl.BlockSpec((tm, o_s.shape[1]), s_row)]
        pair_args += [o_p, o_s]
    return pl.pallas_call(
        functools.partial(_merge_kernel, alpha=alpha, prompt_tiles=p_tiles, n_x=len(xs)),
        grid=(n // tm,),
        in_specs=([pl.BlockSpec((tm, d), row)] if len(xs) == 1 else
                  [pl.BlockSpec((tm, d), p_row), pl.BlockSpec((tm, d), s_row)]) + pair_specs + [
            fixed(w_merge), fixed(w_bp), fixed(w_bg), fixed(w_bn), fixed(w_out), fixed(ln_g), fixed(ln_b),
            fixed(w_router_t), fixed(b_router)],
        out_specs=[pl.BlockSpec((tm, d), row), pl.BlockSpec((N_EXPERTS, tm), lambda i: (0, i))],
        out_shape=[jax.ShapeDtypeStruct((n, d), F32), jax.ShapeDtypeStruct((N_EXPERTS, n), F32)],
        compiler_params=_cparams("parallel"),
        name="merge_ln_route",
    )(*xs, *pair_args, w_merge, w_bp, w_bg, w_bn, w_out, ln_g, ln_b, w_router_t, b_router)


def _moe_kernel(x_ref, c_ref, wg_ref, wu_ref, wd_ref, lg_ref, lb_ref, y_ref, xb_s, acc_s, *, alpha):
    e = pl.program_id(1)

    @pl.when(e == 0)
    def _():
        xb_s[...] = x_ref[...].astype(BF16)
        acc_s[...] = jnp.zeros(acc_s.shape, acc_s.dtype)

    xb = xb_s[...]
    hg = _dot(xb, wg_ref[...])
    hu = _dot(xb, wu_ref[...])
    comb = c_ref[...]
    lane = lax.broadcasted_iota(jnp.int32, comb.shape, 1)
    ce = jnp.sum(jnp.where(lane == e, comb, 0.0), -1, keepdims=True)
    act = (hg * jax.nn.sigmoid(hg)) * hu * ce
    acc_s[...] += _dot(act.astype(BF16), wd_ref[...])

    @pl.when(e == pl.num_programs(1) - 1)
    def _():
        y_ref[...] = _layer_norm(alpha * x_ref[...] + acc_s[...], lg_ref[...], lb_ref[...])


def _moe(x, comb, w_gate, w_up, w_down, layer, ln_g, ln_b, alpha):
    n, d = x.shape
    tm = _tile(n, 1024)
    ne, _, de = w_gate.shape[1:]
    return pl.pallas_call(
        functools.partial(_moe_kernel, alpha=alpha),
        grid=(n // tm, ne),
        in_specs=[pl.BlockSpec((tm, d), lambda i, e: (i, 0)),
                  pl.BlockSpec((tm, ne), lambda i, e: (i, 0)),
                  pl.BlockSpec((None, None, d, de), lambda i, e: (layer, e, 0, 0)),
                  pl.BlockSpec((None, None, d, de), lambda i, e: (layer, e, 0, 0)),
                  pl.BlockSpec((None, None, de, d), lambda i, e: (layer, e, 0, 0)),
                  pl.BlockSpec(ln_g.shape, lambda i, e: (0, 0)),
                  pl.BlockSpec(ln_b.shape, lambda i, e: (0, 0))],
        out_specs=pl.BlockSpec((tm, d), lambda i, e: (i, 0)),
        out_shape=jax.ShapeDtypeStruct((n, d), F32),
        scratch_shapes=[pltpu.VMEM((tm, d), BF16), pltpu.VMEM((tm, d), F32)],
        compiler_params=_cparams("parallel", "arbitrary"),
        name="moe_ln",
    )(x, comb, w_gate, w_up, w_down, ln_g, ln_b)


def _rope_tables(pos):
    half = ROT_DIM // 2
    inv = 1.0 / (ROPE_THETA ** (jnp.arange(half, dtype=F32) / half))
    ang = pos.astype(F32)[:, None] * inv[None, :]
    cos, sin = jnp.cos(ang), jnp.sin(ang)
    n = pos.shape[0]
    one = jnp.ones((n, HEAD_DIM - ROT_DIM), F32)
    zero = jnp.zeros((n, HEAD_DIM - ROT_DIM), F32)
    zh = jnp.zeros((n, half), F32)
    rep = LANES // HEAD_DIM
    cos_t = jnp.tile(jnp.concatenate([cos, cos, one], 1), (1, rep))
    sm_t = jnp.tile(jnp.concatenate([-sin, zh, zero], 1), (1, rep))
    sp_t = jnp.tile(jnp.concatenate([zh, sin, zero], 1), (1, rep))
    return cos_t, sm_t, sp_t


def _overlap(nc, nsb, rows):
    i = np.arange(nc)[:, None]
    j = np.arange(nsb)[None, :]
    lo = np.maximum(i * CMP_STRIDE, j * SEL_LEN)
    hi = np.minimum(i * CMP_STRIDE + CMP_LEN, (j + 1) * SEL_LEN)
    ovl = np.zeros((rows, LANES), np.float32)
    ovl[:nc, :nsb] = np.maximum(hi - lo, 0) // CMP_STRIDE
    return jnp.asarray(ovl, BF16)


def _block_diag(blocks):
    n = len(blocks)
    rows = []
    for i, b in enumerate(blocks):
        rows.append(jnp.concatenate([b if j == i else jnp.zeros((b.shape[0], blocks[j].shape[1]), b.dtype)
                                     for j in range(n)], axis=1))
    return jnp.concatenate(rows, axis=0)


def _cmp_weights(w1, pe):
    w_out, pe_out = [], []
    for kv in range(2):
        w_ab, pe_ab = [], []
        for ab in range(2):
            blocks, pes = [], []
            for r in range(CMP_STRIDE):
                rr = ab * CMP_STRIDE + r
                w_r = w1[kv, rr * HEAD_DIM:(rr + 1) * HEAD_DIM, :]
                blocks.append(_block_diag([w_r] * N_KV_HEADS))
                pes.append(jnp.tile(pe[kv, rr], N_KV_HEADS))
            w_ab.append(jnp.concatenate(blocks, axis=0))
            pe_ab.append(jnp.concatenate(pes)[None, :])
        w_out.append(jnp.stack(w_ab))
        pe_out.append(jnp.stack(pe_ab))
    return jnp.stack(w_out).astype(BF16), jnp.stack(pe_out).astype(F32)


def _chunked_kt(k, nck):
    b, t, h, d = k.shape
    return k.transpose(0, 2, 3, 1).reshape(b, h, d, nck, t // nck).transpose(0, 1, 3, 2, 4).astype(BF16)


def _chunked_v(v, nck):
    b, t, h, d = v.shape
    v = v.transpose(0, 2, 1, 3).reshape(b, h, nck, t // nck, d).astype(BF16)
    return jnp.concatenate([v, jnp.ones_like(v)], axis=-1)


def kernel(x_prompt, x_sample, cache_cmp_kv, cache_sel_kv, state_win_kv, state_pool, page_table, w_in, pool_w, pool_scale, gmlp_ln_g, gmlp_ln_b, gmlp_ws, gmlp_bs, cmp_pe, cmp_w1, cmp_b1, cmp_w2, w_branch_pool, w_branch_gmlp, w_branch_nsa, w_out, ln1_g, ln1_b, ln2_g, ln2_b, w_router, b_router, w_gate, w_up, w_down):
    batch, seq, d = x_prompt.shape
    nb, tq, _ = x_sample.shape
    depth = w_in.shape[0]
    n_pool, page = cache_cmp_kv.shape[1:3]
    npages = page_table.shape[1]
    past = npages * page
    keep = state_win_kv.shape[2]
    pw = pool_scale.shape[-1]
    gw = gmlp_ln_g.shape[-1]
    kvw = N_KV_HEADS * HEAD_DIM
    n_q = N_HEADS * HEAD_DIM
    n_kv = 6 * kvw
    n_auv = pw + 2 * gw
    n_gate = 3 * N_HEADS
    alpha = (2 * depth) ** 0.25
    np_rows, ns_rows = batch * seq, nb * tq
    assert pw == gw and seq % KEY_CHUNK == 0 and seq % Q_BLOCK == 0 and tq <= CHUNK and tq % 8 == 0
    assert keep == WINDOW and keep % page == 0 and page % CMP_STRIDE == 0 and np_rows % tq == 0
    nc_p = (seq - CMP_LEN) // CMP_STRIDE + 1
    nc_s = (past + tq - CMP_LEN) // CMP_STRIDE + 1
    assert nc_p < seq // CMP_STRIDE and nc_s < past // CMP_STRIDE
    nsb_p = -(-seq // SEL_LEN)
    nsb_s = -(-(past + tq) // SEL_LEN)
    assert nsb_p <= SEL_ROWS and nsb_s <= LANES

    w_main = w_in[:, :, :n_auv + n_q + n_kv].astype(BF16)
    w_g = jnp.pad(w_in[:, :, n_auv + n_q + n_kv:n_auv + n_q + n_kv + n_gate], ((0, 0), (0, 0), (0, LANES - n_gate))).astype(BF16)
    w_merge = w_in[:, :, n_auv + n_q + n_kv + n_gate:].astype(BF16)
    w_bp, w_bg, w_bn, w_o = (w.astype(BF16) for w in (w_branch_pool, w_branch_gmlp, w_branch_nsa, w_out))
    wg_b, wu_b, wd_b = w_gate.astype(BF16), w_up.astype(BF16), w_down.astype(BF16)
    w_router_t = w_router.T.astype(F32)
    b_router_c = b_router.astype(F32)[:, None]
    grp_w = gw // GMLP_GROUPS

    pos_all = jnp.concatenate([jnp.tile(jnp.arange(seq), batch), jnp.tile(past + jnp.arange(tq), nb)])
    cos_t, sm_t, sp_t = _rope_tables(pos_all)
    ovl_p = _overlap(nc_p, nsb_p, seq // CMP_STRIDE)
    ovl_s = _overlap(nc_s, nsb_s, past // CMP_STRIDE)
    n_keys_s = (npages + 1) * page
    expand_s = jnp.asarray(np.arange(LANES)[:, None] == (np.arange(n_keys_s)[None, :] // SEL_LEN), BF16)
    expand_p = np.where(np.arange(SEL_ROWS)[:, None] == (np.arange(seq)[None, :] // SEL_LEN), NEG_INF, 0.0)
    expand_p = jnp.asarray(expand_p.reshape(SEL_ROWS, seq // KEY_CHUNK, KEY_CHUNK).transpose(1, 0, 2), BF16)

    xs = (x_prompt.reshape(np_rows, d), x_sample.reshape(ns_rows, d))
    rows_last = (0, 1, 3, 4, 5, 2)
    cache_cmp_t = cache_cmp_kv.transpose(rows_last)
    cache_sel_t = cache_sel_kv.transpose(rows_last)
    state_win_t = state_win_kv.transpose(rows_last)
    nck = seq // KEY_CHUNK

    st_p = [[] for _ in range(4)]
    st_s = [[] for _ in range(5)]
    for l in range(depth):
        z_auv, z_q, z_kv, z_g = _proj(xs, w_main[l], w_g[l], cos_t, sm_t, sp_t, n_auv, n_q, n_kv)

        pool_bd = _block_diag([pool_w[l, g] for g in range(len(POOL_WINDOWS))]).astype(BF16)
        ps, lg, lb = pool_scale[l][None, :], gmlp_ln_g[l][None, :], gmlp_ln_b[l][None, :]
        cl = min(CHUNK, seq)
        bs_rows = jnp.repeat(gmlp_bs[l][:, :cl].T, grp_w, axis=1)
        op_p, og_p = _mix_prompt(z_auv, batch, seq, pool_bd, ps, lg, lb, gmlp_ws[l], bs_rows)
        a_t = z_auv[np_rows:].reshape(nb, tq, n_auv).transpose(1, 0, 2)
        st_t = state_pool[l].transpose(1, 0, 2)
        wrow = jnp.repeat(gmlp_ws[l][:, :tq, :tq].transpose(1, 2, 0), grp_w, axis=2).reshape(tq * tq, gw)
        bsrow = jnp.repeat(gmlp_bs[l][:, :tq].T, grp_w, axis=1)
        op_s, og_s, vn_s = _mix_sample(a_t, st_t, float(past), pool_bd, ps, lg, lb, wrow, bsrow)
        to_rows = lambda y: y.transpose(1, 0, 2).reshape(ns_rows, -1)
        o_pool = (op_p, to_rows(op_s))
        o_gmlp = (og_p, to_rows(og_s))

        w1_bd, pe_rows = _cmp_weights(cmp_w1[l], cmp_pe[l])
        b1 = cmp_b1[l][:, None, :]
        w2kt = cmp_w2[l, 0].T.astype(BF16)
        w2v = cmp_w2[l, 1].astype(BF16)
        h1_p = _cmp1(z_kv, lambda tr: 0, np_rows, w1_bd, pe_rows)
        kct_p, vc_p = _cmp2_prompt(h1_p.reshape(batch, seq // CMP_STRIDE, -1), b1, w2kt, w2v)
        h1_pool = _cmp1_pages(cache_cmp_t, l, w1_bd, pe_rows)
        kct_s, vc_s = _cmp2_sample(page_table, h1_pool.reshape(n_pool, page // CMP_STRIDE, -1), b1, w2kt, w2v)

        zkv_p = z_kv[:np_rows].reshape(batch, seq, 3, 2, N_KV_HEADS, HEAD_DIM)
        o_p = _nsa_prompt(z_q, z_g, kct_p, vc_p,
                          _chunked_kt(zkv_p[:, :, 1, 0], nck), _chunked_v(zkv_p[:, :, 1, 1], nck),
                          _chunked_kt(zkv_p[:, :, 2, 0], nck), _chunked_v(zkv_p[:, :, 2, 1], nck),
                          ovl_p, expand_p, batch, seq)
        o_s = _nsa_sample(page_table, z_q, z_g, z_kv, np_rows, kct_s, vc_s, ovl_s, expand_s, cache_sel_t, state_win_t,
                          l, tq, past, nsb_s)
        o_nsa = (o_p, o_s.astype(BF16))

        x1, comb_t = _merge(xs, o_pool, o_gmlp, o_nsa, w_merge[l], w_bp[l], w_bg[l], w_bn[l], w_o[l],
                            ln1_g[l][None, :], ln1_b[l][None, :], w_router_t, b_router_c, alpha)
        x_all = _moe(x1, comb_t.T, wg_b, wu_b, wd_b, l, ln2_g[l][None, :], ln2_b[l][None, :], alpha)
        xs = (x_all,)

        kv_shape = (2, N_KV_HEADS, HEAD_DIM)
        zkv_s = z_kv[np_rows:].reshape(nb, tq, 3, *kv_shape)
        a_p = z_auv[:np_rows, :pw].reshape(batch, seq, pw)
        a_s = z_auv[np_rows:, :pw].reshape(nb, tq, pw)
        zkv_p = zkv_p.reshape(batch, seq, 3, *kv_shape)
        st_p[0].append(zkv_p[:, :, 0])
        st_p[1].append(zkv_p[:, :, 1])
        st_p[2].append(zkv_p[:, seq - min(WINDOW, seq):, 2])
        st_p[3].append(a_p[:, seq - POOL_KEEP:])
        st_s[0].append(zkv_s[:, :, 0])
        st_s[1].append(zkv_s[:, :, 1])
        st_s[2].append(zkv_s[:, :, 2])
        st_s[3].append(a_s)
        st_s[4].append(vn_s.transpose(1, 0, 2))

    y_prompt = x_all[:np_rows].reshape(batch, seq, d)
    y_sample = x_all[np_rows:].reshape(nb, tq, d)
    new_win = jnp.concatenate([state_win_kv, jnp.stack(st_s[2])], axis=2)[:, :, -keep:]
    new_pool = jnp.concatenate([state_pool, jnp.stack(st_s[3])], axis=2)[:, :, -POOL_KEEP:]
    return (y_prompt, y_sample,
            jnp.stack(st_p[0]), jnp.stack(st_p[1]), jnp.stack(st_p[2]), jnp.stack(st_p[3]),
            jnp.stack(st_s[0]), jnp.stack(st_s[1]), new_win, new_pool, jnp.stack(st_s[4]))
```
